```python
import math
import jax
import jax.numpy as jnp
from jax import lax
import numpy as np

D_MODEL = 4096
BATCH = 4
SEQ = 2048
DEPTH = 4
DEC_BATCH = 8
DEC_SEQ = 8
PAST_LEN = 8192
PAGE_SIZE = 128

HEAD_DIM = 128
N_BRANCH = 4
BRANCH_W = D_MODEL // N_BRANCH
N_HEADS = BRANCH_W // HEAD_DIM
DIFF_DQK = HEAD_DIM // 2
RET_CHUNK = 128
ATTN_Q_BLOCK = 128
MOBA_BLOCK = 256
MOBA_TOPK = 3
MOBA_Q_CHUNK = 32
CONV_W = 4
LRU_BLOCKS = N_HEADS
LRU_BLOCK_W = BRANCH_W // LRU_BLOCKS
LRU_C = 8.0
D_FF = -(-8 * D_MODEL // (3 * 256)) * 256
N_ADA = 6
IN_W = 12 * BRANCH_W + N_BRANCH * D_MODEL
EPS = 1e-6
NEG_INF = -1e30

kernel_name = 'hybrid_retention_diffattn_moba_rglru_decoder_step'


def _rms(x, g):
    xf = x.astype(jnp.float32)
    y = xf * lax.rsqrt(jnp.mean(xf * xf, axis=-1, keepdims=True) + EPS)
    return (y * g.astype(jnp.float32)).astype(x.dtype)


def _alibi_slopes():
    return jnp.asarray(2.0 ** (-8.0 * np.arange(1, N_HEADS + 1) / N_HEADS), dtype=jnp.float32)


def _ret_log_decay():
    return jnp.asarray(np.log(1.0 - 2.0 ** (-5.0 - np.arange(N_HEADS))), dtype=jnp.float32)


def _blockwise(fn, q, qpos, qb):
    b, t = q.shape[:2]
    if t % qb != 0 or t == qb:
        return fn(q, qpos)
    n = t // qb
    qs = jnp.moveaxis(q.reshape((b, n, qb) + q.shape[2:]), 1, 0)
    out = lax.map(lambda a: fn(a[0], a[1]), (qs, qpos.reshape(n, qb)))
    return jnp.moveaxis(out, 0, 1).reshape((b, t) + out.shape[3:])


def _gather_pages(cache_l, page_table):
    pages = cache_l[page_table]
    b, n, p = pages.shape[:3]
    return pages.reshape((b, n * p) + pages.shape[3:])


def retention_branch(q, k, v, g, s0, gn_g):
    b, t, _ = q.shape
    c = RET_CHUNK if t % RET_CHUNK == 0 else t
    n = t // c
    lg = _ret_log_decay()
    shp = (b, n, c, N_HEADS, HEAD_DIM)
    qc = q.reshape(shp).transpose(1, 0, 2, 3, 4)
    kc = (k * HEAD_DIM ** -0.5).reshape(shp).transpose(1, 0, 2, 3, 4)
    vc = v.reshape(shp).transpose(1, 0, 2, 3, 4)
    idx = jnp.arange(c, dtype=jnp.float32)
    dist = idx[:, None] - idx[None, :]
    decay = jnp.where(dist >= 0, jnp.exp(lg[:, None, None] * jnp.maximum(dist, 0.0)), 0.0)
    q_decay = jnp.exp(lg[None, :] * (idx[:, None] + 1.0))
    k_decay = jnp.exp(lg[None, :] * (c - 1.0 - idx[:, None]))
    s_decay = jnp.exp(lg * c)

    def step(s, inp):
        qi, ki, vi = (a.astype(jnp.float32) for a in inp)
        inner = jnp.einsum('bihd,bjhd->bhij', qi, ki) * decay
        o = jnp.einsum('bhij,bjhe->bihe', inner, vi)
        o = o + jnp.einsum('bihd,bhde->bihe', qi, s) * q_decay[None, :, :, None]
        s = s * s_decay[None, :, None, None] + jnp.einsum('bjhd,jh,bjhe->bhde', ki, k_decay, vi)
        return s, o

    s_t, oc = lax.scan(step, s0.astype(jnp.float32), (qc, kc, vc))
    o = oc.transpose(1, 0, 2, 3, 4).reshape(b, t, N_HEADS, HEAD_DIM)
    mu = jnp.mean(o, axis=-1, keepdims=True)
    var = jnp.mean(jnp.square(o - mu), axis=-1, keepdims=True)
    o = (o - mu) * lax.rsqrt(var + EPS) * gn_g.astype(jnp.float32)
    out = o.reshape(b, t, BRANCH_W).astype(q.dtype) * jax.nn.silu(g)
    return out, s_t.astype(q.dtype)


def _diff_attend(q, k, v, qpos, kpos, lam, slopes):
    s = jnp.einsum('bqhmd,bkhmd->bhmqk', q, k).astype(jnp.float32) * DIFF_DQK ** -0.5
    dist = (qpos[:, None] - kpos[None, :]).astype(jnp.float32)
    s = jnp.where(dist >= 0, s - slopes[:, None, None, None] * dist, NEG_INF)
    p = jax.nn.softmax(s, axis=-1)
    a = p[:, :, 0] - lam * p[:, :, 1]
    return jnp.einsum('bhqk,bkhd->bqhd', a.astype(v.dtype), v)


def diff_branch(q, k, v, past_k, past_v, qn_g, kn_g, lam_p, subln_g, lam_init):
    b, t, _ = q.shape
    q = _rms(q.reshape(b, t, N_HEADS, 2, DIFF_DQK), qn_g)
    k = _rms(k.reshape(b, t, N_HEADS, 2, DIFF_DQK), kn_g)
    v = v.reshape(b, t, N_HEADS, HEAD_DIM)
    if past_k is None:
        p_len = 0
        k_all, v_all = k, v
    else:
        p_len = past_k.shape[1]
        k_all = jnp.concatenate([past_k.reshape(b, p_len, N_HEADS, 2, DIFF_DQK).astype(k.dtype), k], axis=1)
        v_all = jnp.concatenate([past_v.astype(v.dtype), v], axis=1)
    lp = lam_p.astype(jnp.float32)
    lam = jnp.exp(jnp.sum(lp[0] * lp[1])) - jnp.exp(jnp.sum(lp[2] * lp[3])) + lam_init
    slopes = _alibi_slopes()
    kpos = jnp.arange(p_len + t)
    qpos = p_len + jnp.arange(t)
    o = _blockwise(lambda qq, pp: _diff_attend(qq, k_all, v_all, pp, kpos, lam, slopes), q, qpos, ATTN_Q_BLOCK)
    o = _rms(o, subln_g) * (1.0 - lam_init)
    return o.reshape(b, t, BRANCH_W), k.reshape(b, t, N_HEADS, 2 * DIFF_DQK), v


def _moba_attend(q, qpos, kb, vb, kmean, slopes):
    b, tq, h, d = q.shape
    nb = kb.shape[2]
    topk = min(MOBA_TOPK, nb)
    qblk = qpos // MOBA_BLOCK
    qt = q.transpose(0, 2, 1, 3)
    gate = jnp.einsum('bhqd,bhnd->bhqn', qt.astype(jnp.float32), kmean)
    past = jnp.arange(nb)[None, :] < qblk[:, None]
    _, sel = lax.top_k(jnp.where(past, gate, NEG_INF), topk)
    sel_ok = sel < qblk[:, None]
    bi = jnp.arange(b)[:, None, None]
    hi = jnp.arange(h)[None, :, None]
    k_sel = kb[bi[..., None], hi[..., None], sel]
    v_sel = vb[bi[..., None], hi[..., None], sel]
    k_own = kb[bi, hi, qblk[None, None, :]]
    v_own = vb[bi, hi, qblk[None, None, :]]
    offs = jnp.arange(MOBA_BLOCK)
    sel_dist = (qpos[:, None, None] - (sel[..., None] * MOBA_BLOCK + offs)).astype(jnp.float32)
    own_dist = (qpos[:, None] - (qblk[:, None] * MOBA_BLOCK + offs)).astype(jnp.float32)
    scale = d ** -0.5
    s_sel = jnp.einsum('bhqd,bhqkjd->bhqkj', qt, k_sel).astype(jnp.float32) * scale - slopes[:, None, None, None] * sel_dist
    s_sel = jnp.where(sel_ok[..., None], s_sel, NEG_INF)
    s_own = jnp.einsum('bhqd,bhqjd->bhqj', qt, k_own).astype(jnp.float32) * scale - slopes[:, None, None] * own_dist
    s_own = jnp.where(own_dist >= 0, s_own, NEG_INF)
    n_sel = topk * MOBA_BLOCK
    p = jax.nn.softmax(jnp.concatenate([s_sel.reshape(b, h, tq, n_sel), s_own], axis=-1), axis=-1).astype(vb.dtype)
    o = jnp.einsum('bhqkj,bhqkjd->bqhd', p[..., :n_sel].reshape(b, h, tq, topk, MOBA_BLOCK), v_sel)
    return o + jnp.einsum('bhqj,bhqjd->bqhd', p[..., n_sel:], v_own)


def moba_branch(q, k, v, past_k, past_v, qn_g, kn_g):
    b, t, _ = q.shape
    q = _rms(q.reshape(b, t, N_HEADS, HEAD_DIM), qn_g)
    k = _rms(k.reshape(b, t, N_HEADS, HEAD_DIM), kn_g)
    v = v.reshape(b, t, N_HEADS, HEAD_DIM)
    if past_k is None:
        p_len = 0
        k_all, v_all = k, v
    else:
        p_len = past_k.shape[1]
        k_all = jnp.concatenate([past_k.astype(k.dtype), k], axis=1)
        v_all = jnp.concatenate([past_v.astype(v.dtype), v], axis=1)
    total = p_len + t
    nb = -(-total // MOBA_BLOCK)
    pad = nb * MOBA_BLOCK - total

    def to_blocks(a):
        a = jnp.pad(a, ((0, 0), (0, pad), (0, 0), (0, 0)))
        return a.reshape(b, nb, MOBA_BLOCK, N_HEADS, HEAD_DIM).transpose(0, 3, 1, 2, 4)

    kb = to_blocks(k_all)
    vb = to_blocks(v_all)
    kmean = jnp.mean(kb.astype(jnp.float32), axis=3)
    slopes = _alibi_slopes()
    qpos = p_len + jnp.arange(t)
    o = _blockwise(lambda qq, pp: _moba_attend(qq, pp, kb, vb, kmean, slopes), q, qpos, MOBA_Q_CHUNK)
    return o.reshape(b, t, BRANCH_W), k, v


def rglru_branch(xr, xg, h0, buf0, conv_w, conv_b, w_a, b_a, w_x, b_x, lam):
    b, t, _ = xr.shape
    xc = jnp.concatenate([buf0.astype(xr.dtype), xr], axis=1)
    y = conv_b + sum(xc[:, i:i + t] * conv_w[i] for i in range(CONV_W))
    yb = y.reshape(b, t, LRU_BLOCKS, LRU_BLOCK_W)
    r = jax.nn.sigmoid(jnp.einsum('btnc,ncd->btnd', yb, w_a).reshape(b, t, BRANCH_W) + b_a)
    i_g = jax.nn.sigmoid(jnp.einsum('btnc,ncd->btnd', yb, w_x).reshape(b, t, BRANCH_W) + b_x)
    log_a = -LRU_C * r.astype(jnp.float32) * jax.nn.softplus(-lam.astype(jnp.float32))
    u = jnp.sqrt(-jnp.expm1(2.0 * log_a)) * (i_g * y).astype(jnp.float32)

    def step(h, inp):
        h = inp[0] * h + inp[1]
        return h, h

    h_t, hs = lax.scan(step, h0.astype(jnp.float32), (jnp.exp(log_a).swapaxes(0, 1), u.swapaxes(0, 1)))
    out = hs.swapaxes(0, 1).astype(xr.dtype) * jax.nn.gelu(xg)
    return out, h_t.astype(xr.dtype), xc[:, -(CONV_W - 1):]


def trunk_layer(x, c, lp, lam_init, past, init_state):
    b, t = x.shape[:2]
    mod = (jax.nn.silu(c) @ lp['w_ada'] + lp['b_ada']).reshape(c.shape[0], N_ADA, D_MODEL)
    shift1, scale1, gate1, shift2, scale2, gate2 = (mod[:, i, None, :] for i in range(N_ADA))
    u = _rms(x, lp['norm1_g']) * (1.0 + scale1) + shift1
    parts = jnp.split(u @ lp['w_in'], [BRANCH_W * i for i in range(1, 13)], axis=-1)
    rq, rk, rv, rg, dq, dk, dv, mq, mk, mv, lx, lg, gates = parts
    if past is None:
        pk_d, pv_d, pk_m, pv_m = None, None, None, None
    else:
        pk_d, pv_d, pk_m, pv_m = past
    ret_s0, lru_h0, conv0 = init_state
    o_ret, ret_s = retention_branch(rq, rk, rv, rg, ret_s0, lp['ret_gn_g'])
    o_dif, dif_k, dif_v = diff_branch(dq, dk, dv, pk_d, pv_d, lp['diff_qn_g'], lp['diff_kn_g'], lp['diff_lam'], lp['diff_subln_g'], lam_init)
    o_moba, moba_k, moba_v = moba_branch(mq, mk, mv, pk_m, pv_m, lp['moba_qn_g'], lp['moba_kn_g'])
    o_lru, lru_h, lru_conv = rglru_branch(lx, lg, lru_h0, conv0, lp['lru_conv_w'], lp['lru_conv_b'], lp['lru_w_a'], lp['lru_b_a'], lp['lru_w_x'], lp['lru_b_x'], lp['lru_lam'])
    gates = jax.nn.sigmoid(gates.reshape(b, t, N_BRANCH, D_MODEL))
    merged = sum(gates[:, :, i] * (o_b @ lp['w_br'][i]) for i, o_b in enumerate((o_ret, o_dif, o_moba, o_lru)))
    x = x + gate1 * (merged @ lp['w_o'])
    u2 = _rms(x, lp['norm2_g']) * (1.0 + scale2) + shift2
    hg, hu = jnp.split(u2 @ lp['w_ffn_in'], 2, axis=-1)
    x = x + gate2 * ((jax.nn.silu(hg) * hu) @ lp['w_ffn_out'])
    return x, (dif_k, dif_v, moba_k, moba_v, ret_s, lru_h, lru_conv)


def setup_inputs(seed: int = 0) -> dict:
    key = jax.random.key(seed)
    ks = jax.random.split(key, 40)

    def nrm(k, shape, s):
        return s * jax.random.normal(k, shape, jnp.float32)

    n_pages = PAST_LEN // PAGE_SIZE
    n_used = DEC_BATCH * n_pages
    n_pool = n_used + max(1, n_used // 4)
    page_table = jax.random.permutation(ks[0], n_pool)[:n_used].reshape(DEC_BATCH, n_pages).astype(jnp.int32)
    a_base = jax.random.uniform(ks[1], (DEPTH, BRANCH_W), jnp.float32, 0.9, 0.999) ** (1.0 / LRU_C)
    lru_lam = jnp.log(a_base) - jnp.log1p(-a_base)
    return {
        'x_prompt': nrm(ks[2], (BATCH, SEQ, D_MODEL), 1.0),
        'x_sample': nrm(ks[3], (DEC_BATCH, DEC_SEQ, D_MODEL), 1.0),
        'c_prompt': nrm(ks[4], (BATCH, D_MODEL), 1.0),
        'c_sample': nrm(ks[5], (DEC_BATCH, D_MODEL), 1.0),
        'cache_dif_k': nrm(ks[6], (DEPTH, n_pool, PAGE_SIZE, N_HEADS, 2 * DIFF_DQK), 1.0),
        'cache_dif_v': nrm(ks[7], (DEPTH, n_pool, PAGE_SIZE, N_HEADS, HEAD_DIM), 1.0),
        'cache_moba_k': nrm(ks[8], (DEPTH, n_pool, PAGE_SIZE, N_HEADS, HEAD_DIM), 1.0),
        'cache_moba_v': nrm(ks[9], (DEPTH, n_pool, PAGE_SIZE, N_HEADS, HEAD_DIM), 1.0),
        'state_ret': nrm(ks[10], (DEPTH, DEC_BATCH, N_HEADS, HEAD_DIM, HEAD_DIM), 1.0),
        'state_lru_h': nrm(ks[11], (DEPTH, DEC_BATCH, BRANCH_W), 0.5),
        'state_lru_conv': nrm(ks[12], (DEPTH, DEC_BATCH, CONV_W - 1, BRANCH_W), 1.0),
        'page_table': page_table,
        'norm1_g': 1.0 + nrm(ks[13], (DEPTH, D_MODEL), 0.02),
        'norm2_g': 1.0 + nrm(ks[14], (DEPTH, D_MODEL), 0.02),
        'w_ada': nrm(ks[15], (DEPTH, D_MODEL, N_ADA * D_MODEL), 0.2 * D_MODEL ** -0.5),
        'b_ada': nrm(ks[16], (DEPTH, N_ADA * D_MODEL), 0.02),
        'w_in': nrm(ks[17], (DEPTH, D_MODEL, IN_W), D_MODEL ** -0.5),
        'ret_gn_g': 1.0 + nrm(ks[18], (DEPTH, HEAD_DIM), 0.02),
        'diff_qn_g': 1.0 + nrm(ks[19], (DEPTH, DIFF_DQK), 0.02),
        'diff_kn_g': 1.0 + nrm(ks[20], (DEPTH, DIFF_DQK), 0.02),
        'diff_lam': nrm(ks[21], (DEPTH, 4, DIFF_DQK), 0.1),
        'diff_subln_g': 1.0 + nrm(ks[22], (DEPTH, HEAD_DIM), 0.02),
        'moba_qn_g': 1.0 + nrm(ks[23], (DEPTH, HEAD_DIM), 0.02),
        'moba_kn_g': 1.0 + nrm(ks[24], (DEPTH, HEAD_DIM), 0.02),
        'lru_conv_w': nrm(ks[25], (DEPTH, CONV_W, BRANCH_W), CONV_W ** -0.5),
        'lru_conv_b': nrm(ks[26], (DEPTH, BRANCH_W), 0.01),
        'lru_w_a': nrm(ks[27], (DEPTH, LRU_BLOCKS, LRU_BLOCK_W, LRU_BLOCK_W), LRU_BLOCK_W ** -0.5),
        'lru_b_a': nrm(ks[28], (DEPTH, BRANCH_W), 0.01),
        'lru_w_x': nrm(ks[29], (DEPTH, LRU_BLOCKS, LRU_BLOCK_W, LRU_BLOCK_W), LRU_BLOCK_W ** -0.5),
        'lru_b_x': nrm(ks[30], (DEPTH, BRANCH_W), 0.01),
        'lru_lam': lru_lam,
        'w_br': nrm(ks[31], (DEPTH, N_BRANCH, BRANCH_W, D_MODEL), BRANCH_W ** -0.5),
        'w_o': nrm(ks[32], (DEPTH, D_MODEL, D_MODEL), D_MODEL ** -0.5),
        'w_ffn_in': nrm(ks[33], (DEPTH, D_MODEL, 2 * D_FF), D_MODEL ** -0.5),
        'w_ffn_out': nrm(ks[34], (DEPTH, D_FF, D_MODEL), D_FF ** -0.5),
    }


def reference(x_prompt, x_sample, c_prompt, c_sample, cache_dif_k, cache_dif_v, cache_moba_k, cache_moba_v, state_ret, state_lru_h, state_lru_conv, page_table, norm1_g, norm2_g, w_ada, b_ada, w_in, ret_gn_g, diff_qn_g, diff_kn_g, diff_lam, diff_subln_g, moba_qn_g, moba_kn_g, lru_conv_w, lru_conv_b, lru_w_a, lru_b_a, lru_w_x, lru_b_x, lru_lam, w_br, w_o, w_ffn_in, w_ffn_out):
    bp = x_prompt.shape[0]
    dt = x_prompt.dtype
    y_p, y_s = x_prompt, x_sample
    new_p = [[] for _ in range(7)]
    new_s = [[] for _ in range(7)]
    for l in range(DEPTH):
        lp = {
            'norm1_g': norm1_g[l], 'norm2_g': norm2_g[l], 'w_ada': w_ada[l], 'b_ada': b_ada[l],
            'w_in': w_in[l], 'ret_gn_g': ret_gn_g[l], 'diff_qn_g': diff_qn_g[l], 'diff_kn_g': diff_kn_g[l],
            'diff_lam': diff_lam[l], 'diff_subln_g': diff_subln_g[l], 'moba_qn_g': moba_qn_g[l],
            'moba_kn_g': moba_kn_g[l], 'lru_conv_w': lru_conv_w[l], 'lru_conv_b': lru_conv_b[l],
            'lru_w_a': lru_w_a[l], 'lru_b_a': lru_b_a[l], 'lru_w_x': lru_w_x[l], 'lru_b_x': lru_b_x[l],
            'lru_lam': lru_lam[l], 'w_br': w_br[l], 'w_o': w_o[l], 'w_ffn_in': w_ffn_in[l],
            'w_ffn_out': w_ffn_out[l],
        }
        lam_init = 0.8 - 0.6 * math.exp(-0.3 * l)
        zero_state = (jnp.zeros((bp, N_HEADS, HEAD_DIM, HEAD_DIM), dt), jnp.zeros((bp, BRANCH_W), dt), jnp.zeros((bp, CONV_W - 1, BRANCH_W), dt))
        y_p, st_p = trunk_layer(y_p, c_prompt, lp, lam_init, None, zero_state)
        past = (_gather_pages(cache_dif_k[l], page_table), _gather_pages(cache_dif_v[l], page_table), _gather_pages(cache_moba_k[l], page_table), _gather_pages(cache_moba_v[l], page_table))
        y_s, st_s = trunk_layer(y_s, c_sample, lp, lam_init, past, (state_ret[l], state_lru_h[l], state_lru_conv[l]))
        for acc, a in zip(new_p, st_p):
            acc.append(a)
        for acc, a in zip(new_s, st_s):
            acc.append(a)
    p_dif_k, p_dif_v, p_moba_k, p_moba_v, p_ret, p_lru_h, p_lru_conv = [jnp.stack(a) for a in new_p]
    s_dif_k, s_dif_v, s_moba_k, s_moba_v, s_ret, s_lru_h, s_lru_conv = [jnp.stack(a) for a in new_s]
    return (y_p, y_s, p_dif_k, p_dif_v, p_moba_k, p_moba_v, p_ret, p_lru_h, p_lru_conv, s_dif_k, s_dif_v, s_moba_k, s_moba_v, s_ret, s_lru_h, s_lru_conv)
```

```python
import functools
import math

import numpy as np
import jax
import jax.numpy as jnp
from jax import lax
from jax.experimental import pallas as pl
from jax.experimental.pallas import tpu as pltpu

F32 = jnp.float32
BF16 = jnp.bfloat16

HEAD_DIM = 128
N_HEADS = 8
BRANCH_W = N_HEADS * HEAD_DIM
N_BRANCH = 4
DIFF_DQK = HEAD_DIM // 2
MOBA_BLOCK = 256
MOBA_TOPK = 3
CONV_W = 4
LRU_C = 8.0
N_ADA = 6
EPS = 1e-6
NEG_INF = -1e30
BELOW_NEG_INF = -3e38
PAGE = 128
VMEM_LIMIT = 56 * 1024 * 1024

C_RQ, C_RK, C_RV, C_RG, C_DQ, C_DK, C_DV, C_MQ, C_MK, C_MV, C_LX, C_LG, C_GATES = range(13)


def _cparams(*sem):
    return pltpu.CompilerParams(dimension_semantics=sem, vmem_limit_bytes=VMEM_LIMIT)


def _sigmoid(x):
    return 1.0 / (1.0 + jnp.exp(-x))


def _dot(a, b):
    return jnp.dot(a, b, preferred_element_type=F32)


def _dot_nt(a, b):
    return lax.dot_general(a, b, (((1,), (1,)), ((), ())), preferred_element_type=F32)


def _dot_tn(a, b):
    return lax.dot_general(a, b, (((0,), (0,)), ((), ())), preferred_element_type=F32)


def _ada_kernel(c_ref, w_ref, b_ref, o_ref):
    c = c_ref[...]
    a = (c * _sigmoid(c)).astype(BF16)
    o_ref[0] = _dot(a, w_ref[0].astype(BF16)) + b_ref[0]


def _ada_mod(c_all, w_ada, b_ada):
    depth, d, n = w_ada.shape
    r = c_all.shape[0]
    tn = 1024
    return pl.pallas_call(
        _ada_kernel,
        grid=(depth, n // tn),
        in_specs=[
            pl.BlockSpec((r, d), lambda l, j: (0, 0)),
            pl.BlockSpec((1, d, tn), lambda l, j: (l, 0, j)),
            pl.BlockSpec((1, 1, tn), lambda l, j: (l, 0, j)),
        ],
        out_specs=pl.BlockSpec((1, r, tn), lambda l, j: (l, 0, j)),
        out_shape=jax.ShapeDtypeStruct((depth, r, n), F32),
        compiler_params=_cparams("parallel", "parallel"),
        name="ada_mod",
    )(c_all, w_ada, b_ada.reshape(depth, 1, n))


def _normmod_kernel(x_ref, g_ref, sc_ref, sh_ref, o_ref):
    x = x_ref[0]
    y = x * lax.rsqrt(jnp.mean(x * x, axis=-1, keepdims=True) + EPS)
    y = y * g_ref[0]
    o_ref[0] = (y * (1.0 + sc_ref[0]) + sh_ref[0]).astype(o_ref.dtype)


def _normmod(x, g_all, layer, scale, shift, tt):
    b, t, d = x.shape
    r = scale.shape[1]
    rb = 1 if r == 1 else tt
    mod_map = (lambda i, j: (i, 0, 0)) if r == 1 else (lambda i, j: (i, j, 0))
    return pl.pallas_call(
        _normmod_kernel,
        grid=(b, t // tt),
        in_specs=[
            pl.BlockSpec((1, tt, d), lambda i, j: (i, j, 0)),
            pl.BlockSpec((1, 1, d), lambda i, j: (layer, 0, 0)),
            pl.BlockSpec((1, rb, d), mod_map),
            pl.BlockSpec((1, rb, d), mod_map),
        ],
        out_specs=pl.BlockSpec((1, tt, d), lambda i, j: (i, j, 0)),
        out_shape=jax.ShapeDtypeStruct((b, t, d), BF16),
        compiler_params=_cparams("parallel", "parallel"),
        name="norm_mod",
    )(x, g_all, scale, shift)


def _mm_kernel(a_ref, b_ref, o_ref, acc_ref, *, nk):
    k = pl.program_id(2)
    d = _dot(a_ref[...], b_ref[0])
    if nk == 1:
        o_ref[...] = d.astype(o_ref.dtype)
        return

    @pl.when(k == 0)
    def _():
        acc_ref[...] = d

    @pl.when(k > 0)
    def _():
        acc_ref[...] += d

    @pl.when(k == nk - 1)
    def _():
        o_ref[...] = acc_ref[...].astype(o_ref.dtype)


def _matmul(a, w_all, layer, tm, tn, tk, out_dtype=F32):
    m, kk = a.shape
    n = w_all.shape[2]
    nk = kk // tk
    return pl.pallas_call(
        functools.partial(_mm_kernel, nk=nk),
        grid=(m // tm, n // tn, nk),
        in_specs=[
            pl.BlockSpec((tm, tk), lambda i, j, k: (i, k)),
            pl.BlockSpec((1, tk, tn), lambda i, j, k: (layer, k, j)),
        ],
        out_specs=pl.BlockSpec((tm, tn), lambda i, j, k: (i, j)),
        out_shape=jax.ShapeDtypeStruct((m, n), out_dtype),
        scratch_shapes=[pltpu.VMEM((tm, tn) if nk > 1 else (8, 128), F32)],
        compiler_params=_cparams("parallel", "parallel", "arbitrary"),
        name="matmul",
    )(a, w_all)


def _mm_resid_kernel(a_ref, b_ref, x_ref, g_ref, o_ref, acc_ref, *, nk):
    k = pl.program_id(2)
    d = _dot(a_ref[...], b_ref[0])

    def finish(acc):
        o_ref[...] = x_ref[...] + g_ref[0] * acc

    if nk == 1:
        finish(d)
        return

    @pl.when(k == 0)
    def _():
        acc_ref[...] = d

    @pl.when(k > 0)
    def _():
        acc_ref[...] += d

    @pl.when(k == nk - 1)
    def _():
        finish(acc_ref[...])


def _matmul_resid(a, w_all, layer, x, gate, rows_per_gate, tm, tn, tk):
    m, kk = a.shape
    n = w_all.shape[2]
    nk = kk // tk
    r = gate.shape[1]
    tiles_per_gate = max(rows_per_gate // tm, 1)
    return pl.pallas_call(
        functools.partial(_mm_resid_kernel, nk=nk),
        grid=(m // tm, n // tn, nk),
        in_specs=[
            pl.BlockSpec((tm, tk), lambda i, j, k: (i, k)),
            pl.BlockSpec((1, tk, tn), lambda i, j, k: (layer, k, j)),
            pl.BlockSpec((tm, tn), lambda i, j, k: (i, j)),
            pl.BlockSpec((1, r, tn), lambda i, j, k: (i // tiles_per_gate, 0, j)),
        ],
        out_specs=pl.BlockSpec((tm, tn), lambda i, j, k: (i, j)),
        out_shape=jax.ShapeDtypeStruct((m, n), F32),
        scratch_shapes=[pltpu.VMEM((tm, tn) if nk > 1 else (8, 128), F32)],
        compiler_params=_cparams("parallel", "parallel", "arbitrary"),
        name="matmul_resid",
    )(a, w_all, x, gate)


def _mm_swiglu_kernel(a_ref, bg_ref, bu_ref, o_ref):
    a = a_ref[...]
    hg = _dot(a, bg_ref[0])
    hu = _dot(a, bu_ref[0])
    o_ref[...] = (hg * _sigmoid(hg) * hu).astype(o_ref.dtype)


def _matmul_swiglu(a, w_all, layer, tm, tn):
    m, kk = a.shape
    f = w_all.shape[2] // 2
    nj = f // tn
    return pl.pallas_call(
        _mm_swiglu_kernel,
        grid=(m // tm, nj),
        in_specs=[
            pl.BlockSpec((tm, kk), lambda i, j: (i, 0)),
            pl.BlockSpec((1, kk, tn), lambda i, j: (layer, 0, j)),
            pl.BlockSpec((1, kk, tn), lambda i, j: (layer, 0, j + nj)),
        ],
        out_specs=pl.BlockSpec((tm, tn), lambda i, j: (i, j)),
        out_shape=jax.ShapeDtypeStruct((m, f), BF16),
        compiler_params=_cparams("parallel", "parallel"),
        name="matmul_swiglu",
    )(a, w_all, w_all)


def _merge_kernel(o0, o1, o2, o3, g0, g1, g2, g3, w_ref, out_ref):
    acc = None
    for i, (o, g) in enumerate(((o0, g0), (o1, g1), (o2, g2), (o3, g3))):
        t = _sigmoid(g[...]) * _dot(o[...].astype(BF16), w_ref[0, i])
        acc = t if acc is None else acc + t
    out_ref[...] = acc.astype(out_ref.dtype)


def _merge(branch_outs, proj, w_br_all, layer, tm, tn):
    m = proj.shape[0]
    d = w_br_all.shape[3]
    gate_blk0 = C_GATES * BRANCH_W // tn
    per_branch = d // tn
    o_specs = [pl.BlockSpec((tm, BRANCH_W), lambda i, j: (i, 0)) for _ in range(N_BRANCH)]
    g_specs = [
        pl.BlockSpec((tm, tn), functools.partial(lambda i, j, br: (i, gate_blk0 + br * per_branch + j), br=br))
        for br in range(N_BRANCH)
    ]
    return pl.pallas_call(
        _merge_kernel,
        grid=(m // tm, d // tn),
        in_specs=o_specs + g_specs + [pl.BlockSpec((1, N_BRANCH, BRANCH_W, tn), lambda i, j: (layer, 0, 0, j))],
        out_specs=pl.BlockSpec((tm, tn), lambda i, j: (i, j)),
        out_shape=jax.ShapeDtypeStruct((m, d), BF16),
        compiler_params=_cparams("parallel", "parallel"),
        name="merge",
    )(*branch_outs, proj, proj, proj, proj, w_br_all)


def _headnorm_kernel(x_ref, g_ref, o_ref, *, group):
    x = x_ref[0]
    tt = x.shape[0]
    g = g_ref[0]
    for h in range(N_HEADS):
        xh = x[:, h * HEAD_DIM:(h + 1) * HEAD_DIM]
        x2 = xh * xh
        if group == HEAD_DIM:
            ms = jnp.mean(x2, axis=-1, keepdims=True)
        else:
            lo = lax.broadcasted_iota(jnp.int32, (tt, HEAD_DIM), 1) < group
            s_lo = jnp.sum(jnp.where(lo, x2, 0.0), axis=-1, keepdims=True)
            s_hi = jnp.sum(jnp.where(lo, 0.0, x2), axis=-1, keepdims=True)
            ms = jnp.where(lo, s_lo, s_hi) * (1.0 / group)
        o_ref[0, :, h * HEAD_DIM:(h + 1) * HEAD_DIM] = xh * lax.rsqrt(ms + EPS) * g


def _headnorm(proj, col, g_all, layer, group, tt):
    b, t, _ = proj.shape
    return pl.pallas_call(
        functools.partial(_headnorm_kernel, group=group),
        grid=(b, t // tt),
        in_specs=[
            pl.BlockSpec((1, tt, BRANCH_W), lambda i, j: (i, j, col)),
            pl.BlockSpec((1, 1, HEAD_DIM), lambda i, j: (layer, 0, 0)),
        ],
        out_specs=pl.BlockSpec((1, tt, BRANCH_W), lambda i, j: (i, j, 0)),
        out_shape=jax.ShapeDtypeStruct((b, t, BRANCH_W), F32),
        compiler_params=_cparams("parallel", "parallel"),
        name="head_norm",
    )(proj, g_all)


def _ret_kernel(lg_ref, q_ref, k_ref, v_ref, g_ref, s0_ref, gn_ref, o_ref, s_out_ref, s_scr, *, c, nc):
    ci = pl.program_id(2)

    @pl.when(ci == 0)
    def _():
        s_scr[...] = s0_ref[0, 0]

    lg = lg_ref[0][:, :1]
    ii = lax.broadcasted_iota(jnp.int32, (c, c), 0)
    jj = lax.broadcasted_iota(jnp.int32, (c, c), 1)
    dist = (ii - jj).astype(F32)
    decay = jnp.where(dist >= 0, jnp.exp(lg * jnp.maximum(dist, 0.0)), 0.0)
    idx = lax.broadcasted_iota(jnp.int32, (c, 1), 0).astype(F32)
    q_decay = jnp.exp(lg * (idx + 1.0))
    k_decay = jnp.exp(lg * (c - 1.0 - idx))
    s_decay = jnp.exp(lg * float(c))

    q = q_ref[0]
    k = k_ref[0] * HEAD_DIM ** -0.5
    v = v_ref[0]
    qb = q.astype(BF16)
    vb = v.astype(BF16)
    s = s_scr[...]
    inner = _dot_nt(qb, k.astype(BF16)) * decay
    o = _dot(inner.astype(BF16), vb) + _dot(qb, s.astype(BF16)) * q_decay
    s_new = s * s_decay + _dot_tn((k * k_decay).astype(BF16), vb)
    s_scr[...] = s_new

    @pl.when(ci == nc - 1)
    def _():
        s_out_ref[0, 0] = s_new

    mu = jnp.mean(o, axis=-1, keepdims=True)
    var = jnp.mean(jnp.square(o - mu), axis=-1, keepdims=True)
    o = (o - mu) * lax.rsqrt(var + EPS) * gn_ref[0]
    g = g_ref[0]
    o_ref[0] = (o * (g * _sigmoid(g))).astype(o_ref.dtype)


def _retention(proj, s0, gn_all, layer, c):
    b, t, _ = proj.shape
    nc = t // c
    lg = np.log(1.0 - 2.0 ** (-5.0 - np.arange(N_HEADS)))
    lg_tab = jnp.asarray(np.broadcast_to(lg[:, None, None], (N_HEADS, 1, HEAD_DIM)), dtype=F32)

    def col_spec(col):
        return pl.BlockSpec((1, c, HEAD_DIM), lambda i, h, j: (i, j, col * N_HEADS + h))

    return pl.pallas_call(
        functools.partial(_ret_kernel, c=c, nc=nc),
        grid=(b, N_HEADS, nc),
        in_specs=[
            pl.BlockSpec((1, 1, HEAD_DIM), lambda i, h, j: (h, 0, 0)),
            col_spec(C_RQ), col_spec(C_RK), col_spec(C_RV), col_spec(C_RG),
            pl.BlockSpec((1, 1, HEAD_DIM, HEAD_DIM), lambda i, h, j: (i, h, 0, 0)),
            pl.BlockSpec((1, 1, HEAD_DIM), lambda i, h, j: (layer, 0, 0)),
        ],
        out_specs=[
            pl.BlockSpec((1, c, HEAD_DIM), lambda i, h, j: (i, j, h)),
            pl.BlockSpec((1, 1, HEAD_DIM, HEAD_DIM), lambda i, h, j: (i, h, 0, 0)),
        ],
        out_shape=[
            jax.ShapeDtypeStruct((b, t, BRANCH_W), BF16 if c % 16 == 0 else F32),
            jax.ShapeDtypeStruct((b, N_HEADS, HEAD_DIM, HEAD_DIM), F32),
        ],
        scratch_shapes=[pltpu.VMEM((HEAD_DIM, HEAD_DIM), F32)],
        compiler_params=_cparams("parallel", "parallel", "arbitrary"),
        name="retention",
    )(lg_tab, proj, proj, proj, proj, s0, gn_all)


def _alibi_slopes():
    return 2.0 ** (-8.0 * np.arange(1, N_HEADS + 1) / N_HEADS)


def _diff_lambda(lam_ref, lam_init):
    lp = lam_ref[0]
    s01 = jnp.sum(lp[0:1] * lp[1:2], axis=-1, keepdims=True)
    s23 = jnp.sum(lp[2:3] * lp[3:4], axis=-1, keepdims=True)
    return jnp.exp(s01) - jnp.exp(s23) + lam_init


def _softmax_update(s, v_bf16, m_ref, l_ref, acc_ref):
    m_old = m_ref[...]
    m_new = jnp.maximum(m_old, jnp.max(s, axis=-1, keepdims=True))
    alpha = jnp.exp(m_old - m_new)
    p = jnp.exp(s - m_new)
    l_ref[...] = alpha * l_ref[...] + jnp.sum(p, axis=-1, keepdims=True)
    acc_ref[...] = alpha * acc_ref[...] + _dot(p.astype(BF16), v_bf16)
    m_ref[...] = m_new


def _subln(o, g, lam_init):
    return o * lax.rsqrt(jnp.mean(o * o, axis=-1, keepdims=True) + EPS) * g * (1.0 - lam_init)


def _diff_attn_kernel(sl_ref, lam_ref, q_ref, k_ref, v_ref, sg_ref, o_ref,
                      m0, l0, a0, m1, l1, a1, *, tq, lam_init):
    qi = pl.program_id(2)
    slope = sl_ref[0][:, :1]
    q = q_ref[0] * DIFF_DQK ** -0.5
    lane = lax.broadcasted_iota(jnp.int32, (tq, HEAD_DIM), 1)
    q0 = jnp.where(lane < DIFF_DQK, q, 0.0).astype(BF16)
    q1 = jnp.where(lane < DIFF_DQK, 0.0, q).astype(BF16)
    for m_ref, l_ref, a_ref in ((m0, l0, a0), (m1, l1, a1)):
        m_ref[...] = jnp.full(m_ref.shape, NEG_INF, F32)
        l_ref[...] = jnp.zeros(l_ref.shape, F32)
        a_ref[...] = jnp.zeros(a_ref.shape, F32)
    rel = lax.broadcasted_iota(jnp.int32, (tq, tq), 0) - lax.broadcasted_iota(jnp.int32, (tq, tq), 1)

    def tile(j, masked):
        start = pl.multiple_of(j * tq, tq)
        kb = k_ref[0, pl.ds(start, tq), :].astype(BF16)
        vb = v_ref[0, pl.ds(start, tq), :].astype(BF16)
        dist = (rel + (qi - j) * tq).astype(F32)
        bias = slope * dist
        for qm, m_ref, l_ref, a_ref in ((q0, m0, l0, a0), (q1, m1, l1, a1)):
            s = _dot_nt(qm, kb) - bias
            if masked:
                s = jnp.where(dist >= 0, s, NEG_INF)
            _softmax_update(s, vb, m_ref, l_ref, a_ref)

    tile(qi, True)

    def body(j, carry):
        tile(j, False)
        return carry

    lax.fori_loop(0, qi, body, 0)
    lam = _diff_lambda(lam_ref, lam_init)
    o = a0[...] / l0[...] - lam * (a1[...] / l1[...])
    o_ref[0] = _subln(o, sg_ref[0], lam_init).astype(o_ref.dtype)


def _diff_attn(qn, kn, proj, lam_all, sg_all, layer, lam_init, tq):
    b, t, _ = qn.shape
    sl_tab = jnp.asarray(np.broadcast_to(_alibi_slopes()[:, None, None], (N_HEADS, 1, HEAD_DIM)), dtype=F32)
    stat = pltpu.VMEM((tq, 1), F32)
    acc = pltpu.VMEM((tq, HEAD_DIM), F32)
    return pl.pallas_call(
        functools.partial(_diff_attn_kernel, tq=tq, lam_init=lam_init),
        grid=(b, N_HEADS, t // tq),
        in_specs=[
            pl.BlockSpec((1, 1, HEAD_DIM), lambda i, h, j: (h, 0, 0)),
            pl.BlockSpec((1, 4, DIFF_DQK), lambda i, h, j: (layer, 0, 0)),
            pl.BlockSpec((1, tq, HEAD_DIM), lambda i, h, j: (i, j, h)),
            pl.BlockSpec((1, t, HEAD_DIM), lambda i, h, j: (i, 0, h)),
            pl.BlockSpec((1, t, HEAD_DIM), lambda i, h, j: (i, 0, C_DV * N_HEADS + h)),
            pl.BlockSpec((1, 1, HEAD_DIM), lambda i, h, j: (layer, 0, 0)),
        ],
        out_specs=pl.BlockSpec((1, tq, HEAD_DIM), lambda i, h, j: (i, j, h)),
        out_shape=jax.ShapeDtypeStruct((b, t, BRANCH_W), BF16),
        scratch_shapes=[stat, stat, acc, stat, stat, acc],
        compiler_params=_cparams("parallel", "parallel", "arbitrary"),
        name="diff_attn",
    )(sl_tab, lam_all, qn, kn, proj, sg_all)


def _split_bf16(x):
    hi = x.astype(BF16)
    lo = (x - hi.astype(F32)).astype(BF16)
    return hi, lo


def _moba_attn_kernel(sl_ref, q_ref, k_ref, v_ref, o_ref, kmean, m_s, l_s, acc, *, nb):
    tq = MOBA_BLOCK
    qi = pl.program_id(2)
    slope = sl_ref[0][:, :1]

    @pl.when(qi == 0)
    def _():
        kmean[...] = jnp.zeros(kmean.shape, F32)
        for n in range(nb):
            kmean[n:n + 1, :] = jnp.mean(k_ref[0, n * tq:(n + 1) * tq, :], axis=0, keepdims=True)

    q = q_ref[0]
    qb = q.astype(BF16)
    qh, ql = _split_bf16(q)
    kh, kl = _split_bf16(kmean[...])
    gate = _dot_nt(qh, kh) + (_dot_nt(qh, kl) + _dot_nt(ql, kh))
    lane = lax.broadcasted_iota(jnp.int32, (tq, HEAD_DIM), 1)
    g = jnp.where(lane < qi, gate, NEG_INF)
    allowed = jnp.zeros((tq, HEAD_DIM), F32)
    for _ in range(MOBA_TOPK):
        mx = jnp.max(g, axis=-1, keepdims=True)
        first = jnp.min(jnp.where(g == mx, lane, HEAD_DIM), axis=-1, keepdims=True)
        hit = lane == first
        allowed = jnp.where(hit & (mx > 0.5 * NEG_INF), 1.0, allowed)
        g = jnp.where(hit, BELOW_NEG_INF, g)

    m_s[...] = jnp.full(m_s.shape, NEG_INF, F32)
    l_s[...] = jnp.zeros(l_s.shape, F32)
    acc[...] = jnp.zeros(acc.shape, F32)
    rel = lax.broadcasted_iota(jnp.int32, (tq, tq), 0) - lax.broadcasted_iota(jnp.int32, (tq, tq), 1)
    scale = HEAD_DIM ** -0.5

    def tile(j, own):
        start = pl.multiple_of(j * tq, tq)
        kb = k_ref[0, pl.ds(start, tq), :].astype(BF16)
        vb = v_ref[0, pl.ds(start, tq), :].astype(BF16)
        dist = (rel + (qi - j) * tq).astype(F32)
        s = _dot_nt(qb, kb) * scale - slope * dist
        if own:
            s = jnp.where(dist >= 0, s, NEG_INF)
        else:
            picked = jnp.sum(jnp.where(lane == j, allowed, 0.0), axis=-1, keepdims=True) > 0.0
            s = jnp.where(picked, s, NEG_INF)
        _softmax_update(s, vb, m_s, l_s, acc)

    tile(qi, True)

    def body(j, carry):
        tile(j, False)
        return carry

    lax.fori_loop(0, qi, body, 0)
    o_ref[0] = (acc[...] / l_s[...]).astype(o_ref.dtype)


def _moba_attn(qn, kn, proj, tq_unused=None):
    b, t, _ = qn.shape
    assert t % MOBA_BLOCK == 0
    nb = t // MOBA_BLOCK
    assert nb <= HEAD_DIM
    sl_tab = jnp.asarray(np.broadcast_to(_alibi_slopes()[:, None, None], (N_HEADS, 1, HEAD_DIM)), dtype=F32)
    return pl.pallas_call(
        functools.partial(_moba_attn_kernel, nb=nb),
        grid=(b, N_HEADS, nb),
        in_specs=[
            pl.BlockSpec((1, 1, HEAD_DIM), lambda i, h, j: (h, 0, 0)),
            pl.BlockSpec((1, MOBA_BLOCK, HEAD_DIM), lambda i, h, j: (i, j, h)),
            pl.BlockSpec((1, t, HEAD_DIM), lambda i, h, j: (i, 0, h)),
            pl.BlockSpec((1, t, HEAD_DIM), lambda i, h, j: (i, 0, C_MV * N_HEADS + h)),
        ],
        out_specs=pl.BlockSpec((1, MOBA_BLOCK, HEAD_DIM), lambda i, h, j: (i, j, h)),
        out_shape=jax.ShapeDtypeStruct((b, t, BRANCH_W), BF16),
        scratch_shapes=[
            pltpu.VMEM((HEAD_DIM, HEAD_DIM), F32),
            pltpu.VMEM((MOBA_BLOCK, 1), F32),
            pltpu.VMEM((MOBA_BLOCK, 1), F32),
            pltpu.VMEM((MOBA_BLOCK, HEAD_DIM), F32),
        ],
        compiler_params=_cparams("parallel", "parallel", "arbitrary"),
        name="moba_attn",
    )(sl_tab, qn, kn, proj)


def _query_rows(q, maps):
    t = q.shape[0]
    rows = N_HEADS * maps * t
    width = HEAD_DIM // maps
    tiled = jnp.concatenate([q] * (N_HEADS * maps), axis=0)
    r = lax.broadcasted_iota(jnp.int32, (rows, BRANCH_W), 0)
    ch = lax.broadcasted_iota(jnp.int32, (rows, BRANCH_W), 1)
    return jnp.where(r // t == ch // width, tiled, 0.0)


def _diff_dec_kernel(pt_ref, sl_ref, lam_ref, q_ref, kc_ref, vc_ref, kn_ref, vn_ref, sg_ref, o_ref,
                     qrows, m_s, l_s, acc, *, t, n_pages, lam_init):
    p = pl.program_id(1)
    rows = N_HEADS * 2 * t
    p_len = n_pages * PAGE

    @pl.when(p == 0)
    def _():
        qrows[...] = _query_rows(q_ref[0] * DIFF_DQK ** -0.5, 2).astype(BF16)
        m_s[...] = jnp.full(m_s.shape, NEG_INF, F32)
        l_s[...] = jnp.zeros(l_s.shape, F32)
        acc[...] = jnp.zeros(acc.shape, F32)

    slope = sl_ref[:, :1]
    qpos = p_len + lax.broadcasted_iota(jnp.int32, (rows, PAGE), 0) % t
    tok = lax.broadcasted_iota(jnp.int32, (rows, PAGE), 1)

    dist = (qpos - (p * PAGE + tok)).astype(F32)
    s = _dot_nt(qrows[...], kc_ref[0, 0].astype(BF16)) - slope * dist
    _softmax_update(s, vc_ref[0, 0].astype(BF16), m_s, l_s, acc)

    @pl.when(p == n_pages - 1)
    def _():
        pad = jnp.zeros((PAGE - t, BRANCH_W), F32)
        kn = jnp.concatenate([kn_ref[0], pad], axis=0).astype(BF16)
        vn = jnp.concatenate([vn_ref[0], pad], axis=0).astype(BF16)
        dist_n = (qpos - (p_len + tok)).astype(F32)
        s_n = _dot_nt(qrows[...], kn) - slope * dist_n
        s_n = jnp.where((dist_n >= 0) & (tok < t), s_n, NEG_INF)
        _softmax_update(s_n, vn, m_s, l_s, acc)
        lam = _diff_lambda(lam_ref, lam_init)
        out = acc[...] / l_s[...]
        for h in range(N_HEADS):
            blk = out[h * 2 * t:(h + 1) * 2 * t, h * HEAD_DIM:(h + 1) * HEAD_DIM]
            o = blk[0:t] - lam * blk[t:2 * t]
            o_ref[0, :, h * HEAD_DIM:(h + 1) * HEAD_DIM] = _subln(o, sg_ref[0], lam_init)


def _diff_decode(qn, kn, proj, cache_k, cache_v, page_table, lam_all, sg_all, layer, lam_init):
    b, t, _ = qn.shape
    n_pages = page_table.shape[1]
    rows = N_HEADS * 2 * t
    slopes = np.repeat(_alibi_slopes(), 2 * t)
    sl_tab = jnp.asarray(np.broadcast_to(slopes[:, None], (rows, HEAD_DIM)), dtype=F32)
    grid_spec = pltpu.PrefetchScalarGridSpec(
        num_scalar_prefetch=1,
        grid=(b, n_pages),
        in_specs=[
            pl.BlockSpec((rows, HEAD_DIM), lambda i, p, pt: (0, 0)),
            pl.BlockSpec((1, 4, DIFF_DQK), lambda i, p, pt: (layer, 0, 0)),
            pl.BlockSpec((1, t, BRANCH_W), lambda i, p, pt: (i, 0, 0)),
            pl.BlockSpec((1, 1, PAGE, BRANCH_W), lambda i, p, pt: (layer, pt[i * n_pages + p], 0, 0)),
            pl.BlockSpec((1, 1, PAGE, BRANCH_W), lambda i, p, pt: (layer, pt[i * n_pages + p], 0, 0)),
            pl.BlockSpec((1, t, BRANCH_W), lambda i, p, pt: (i, 0, 0)),
            pl.BlockSpec((1, t, BRANCH_W), lambda i, p, pt: (i, 0, C_DV)),
            pl.BlockSpec((1, 1, HEAD_DIM), lambda i, p, pt: (layer, 0, 0)),
        ],
        out_specs=pl.BlockSpec((1, t, BRANCH_W), lambda i, p, pt: (i, 0, 0)),
        scratch_shapes=[
            pltpu.VMEM((rows, BRANCH_W), BF16),
            pltpu.VMEM((rows, 1), F32),
            pltpu.VMEM((rows, 1), F32),
            pltpu.VMEM((rows, BRANCH_W), F32),
        ],
    )
    return pl.pallas_call(
        functools.partial(_diff_dec_kernel, t=t, n_pages=n_pages, lam_init=lam_init),
        grid_spec=grid_spec,
        out_shape=jax.ShapeDtypeStruct((b, t, BRANCH_W), F32),
        compiler_params=_cparams("parallel", "arbitrary"),
        name="diff_decode",
    )(page_table.reshape(-1), sl_tab, lam_all, qn, cache_k, cache_v, kn, proj, sg_all)


def _diag_blocks(x, t):
    return jnp.concatenate(
        [x[h * t:(h + 1) * t, h * HEAD_DIM:(h + 1) * HEAD_DIM] for h in range(N_HEADS)], axis=0)


def _moba_dec_kernel(pt_ref, sl_ref, q_ref, kc_ref, vc_ref, kn_ref, vn_ref, o_ref,
                     qrows, gsum, m_pg, l_pg, acc_pg, *, t, n_pages):
    p = pl.program_id(1)
    rows = N_HEADS * t
    p_len = n_pages * PAGE
    n_blk = n_pages // 2
    scale = HEAD_DIM ** -0.5
    wide = (rows, HEAD_DIM)

    @pl.when(p == 0)
    def _():
        qrows[...] = _query_rows(q_ref[0], 1).astype(BF16)

    slope = sl_ref[:, :1]
    qpos = p_len + lax.broadcasted_iota(jnp.int32, (rows, PAGE), 0) % t
    tok = lax.broadcasted_iota(jnp.int32, (rows, PAGE), 1)

    def partial_softmax(raw, dist, mask, v_bf16):
        s = raw * scale - slope * dist
        if mask is not None:
            s = jnp.where(mask, s, NEG_INF)
        m = jnp.max(s, axis=-1, keepdims=True)
        e = jnp.exp(s - m)
        if mask is not None:
            e = jnp.where(mask, e, 0.0)
        l = jnp.sum(e, axis=-1, keepdims=True)
        pv = _diag_blocks(_dot(e.astype(BF16), v_bf16), t)
        return m, l, pv

    raw = _dot_nt(qrows[...], kc_ref[0, 0].astype(BF16))
    dist = (qpos - (p * PAGE + tok)).astype(F32)
    m, l, pv = partial_softmax(raw, dist, None, vc_ref[0, 0].astype(BF16))
    gsum[p] = jnp.broadcast_to(jnp.sum(raw, axis=-1, keepdims=True), wide)
    m_pg[p] = jnp.broadcast_to(m, wide)
    l_pg[p] = jnp.broadcast_to(l, wide)
    acc_pg[p] = pv

    @pl.when(p == n_pages - 1)
    def _():
        pad = jnp.zeros((PAGE - t, BRANCH_W), F32)
        kn = jnp.concatenate([kn_ref[0], pad], axis=0).astype(BF16)
        vn = jnp.concatenate([vn_ref[0], pad], axis=0).astype(BF16)
        dist_n = (qpos - (p_len + tok)).astype(F32)
        mask_n = (dist_n >= 0) & (tok < t)
        m_n, l_n, pv_n = partial_softmax(_dot_nt(qrows[...], kn), dist_n, mask_n, vn)
        m_n = jnp.broadcast_to(m_n, wide)
        l_n = jnp.broadcast_to(l_n, wide)

        gates = [(gsum[2 * n] + gsum[2 * n + 1]) * (1.0 / MOBA_BLOCK) for n in range(n_blk)]
        sel = [jnp.zeros(wide, jnp.bool_) for _ in range(n_blk)]
        for _ in range(min(MOBA_TOPK, n_blk)):
            mx = functools.reduce(jnp.maximum, gates)
            found = jnp.zeros(wide, jnp.bool_)
            for n in range(n_blk):
                hit = (gates[n] == mx) & jnp.logical_not(found)
                found = found | hit
                sel[n] = sel[n] | hit
                gates[n] = jnp.where(hit, BELOW_NEG_INF, gates[n])

        m_tot = m_n
        for n in range(n_blk):
            for pg in (2 * n, 2 * n + 1):
                m_tot = jnp.maximum(m_tot, jnp.where(sel[n], m_pg[pg], NEG_INF))
        w_n = jnp.exp(m_n - m_tot)
        l_tot = l_n * w_n
        o = pv_n * w_n
        for n in range(n_blk):
            for pg in (2 * n, 2 * n + 1):
                w = jnp.where(sel[n], jnp.exp(m_pg[pg] - m_tot), 0.0)
                l_tot = l_tot + l_pg[pg] * w
                o = o + acc_pg[pg] * w
        o = o / l_tot
        for h in range(N_HEADS):
            o_ref[0, :, h * HEAD_DIM:(h + 1) * HEAD_DIM] = o[h * t:(h + 1) * t]


def _moba_decode(qn, kn, proj, cache_k, cache_v, page_table, layer):
    b, t, _ = qn.shape
    n_pages = page_table.shape[1]
    assert (n_pages * PAGE) % MOBA_BLOCK == 0 and t <= MOBA_BLOCK and MOBA_BLOCK == 2 * PAGE
    rows = N_HEADS * t
    slopes = np.repeat(_alibi_slopes(), t)
    sl_tab = jnp.asarray(np.broadcast_to(slopes[:, None], (rows, HEAD_DIM)), dtype=F32)
    stats = pltpu.VMEM((n_pages, rows, HEAD_DIM), F32)
    grid_spec = pltpu.PrefetchScalarGridSpec(
        num_scalar_prefetch=1,
        grid=(b, n_pages),
        in_specs=[
            pl.BlockSpec((rows, HEAD_DIM), lambda i, p, pt: (0, 0)),
            pl.BlockSpec((1, t, BRANCH_W), lambda i, p, pt: (i, 0, 0)),
            pl.BlockSpec((1, 1, PAGE, BRANCH_W), lambda i, p, pt: (layer, pt[i * n_pages + p], 0, 0)),
            pl.BlockSpec((1, 1, PAGE, BRANCH_W), lambda i, p, pt: (layer, pt[i * n_pages + p], 0, 0)),
            pl.BlockSpec((1, t, BRANCH_W), lambda i, p, pt: (i, 0, 0)),
            pl.BlockSpec((1, t, BRANCH_W), lambda i, p, pt: (i, 0, C_MV)),
        ],
        out_specs=pl.BlockSpec((1, t, BRANCH_W), lambda i, p, pt: (i, 0, 0)),
        scratch_shapes=[pltpu.VMEM((rows, BRANCH_W), BF16), stats, stats, stats, stats],
    )
    return pl.pallas_call(
        functools.partial(_moba_dec_kernel, t=t, n_pages=n_pages),
        grid_spec=grid_spec,
        out_shape=jax.ShapeDtypeStruct((b, t, BRANCH_W), F32),
        compiler_params=_cparams("parallel", "arbitrary"),
        name="moba_decode",
    )(page_table.reshape(-1), sl_tab, qn, cache_k, cache_v, kn, proj)


CONV_PAD = 8


def _lru_gate_kernel(x_ref, buf_ref, cw_ref, cb_ref, wa_ref, ba_ref, wx_ref, bx_ref, lam_ref,
                     a_ref, u_ref, xe, *, tt):
    ti = pl.program_id(1)

    @pl.when(ti == 0)
    def _():
        xe[0:CONV_PAD, :] = buf_ref[0]

    @pl.when(ti > 0)
    def _():
        xe[0:CONV_PAD, :] = xe[tt:tt + CONV_PAD, :]

    xe[CONV_PAD:CONV_PAD + tt, :] = x_ref[0]
    cw = cw_ref[0]
    y = cb_ref[0]
    for i in range(CONV_W):
        off = CONV_PAD - (CONV_W - 1) + i
        y = y + xe[off:off + tt, :] * cw[i:i + 1]
    neg_lam = -lam_ref[0]
    softplus = jnp.maximum(neg_lam, 0.0) + jnp.log1p(jnp.exp(-jnp.abs(neg_lam)))
    for n in range(N_HEADS):
        sl = slice(n * HEAD_DIM, (n + 1) * HEAD_DIM)
        yn = y[:, sl]
        yb = yn.astype(BF16)
        r = _sigmoid(_dot(yb, wa_ref[0, n]) + ba_ref[0][:, sl])
        ig = _sigmoid(_dot(yb, wx_ref[0, n]) + bx_ref[0][:, sl])
        log_a = -LRU_C * r * softplus[:, sl]
        a = jnp.exp(log_a)
        a_ref[0, :, sl] = a
        u_ref[0, :, sl] = jnp.sqrt(jnp.tanh(-log_a) * (a * a + 1.0)) * (ig * yn)


def _lru_gates(proj, buf_pad, lw, layer, tt):
    b, t, _ = proj.shape
    vec = pl.BlockSpec((1, 1, BRANCH_W), lambda i, j: (layer, 0, 0))
    blk = pl.BlockSpec((1, N_HEADS, HEAD_DIM, HEAD_DIM), lambda i, j: (layer, 0, 0, 0))
    out = pl.BlockSpec((1, tt, BRANCH_W), lambda i, j: (i, j, 0))
    return pl.pallas_call(
        functools.partial(_lru_gate_kernel, tt=tt),
        grid=(b, t // tt),
        in_specs=[
            pl.BlockSpec((1, tt, BRANCH_W), lambda i, j: (i, j, C_LX)),
            pl.BlockSpec((1, CONV_PAD, BRANCH_W), lambda i, j: (i, 0, 0)),
            pl.BlockSpec((1, CONV_W, BRANCH_W), lambda i, j: (layer, 0, 0)),
            vec, blk, vec, blk, vec, vec,
        ],
        out_specs=[out, out],
        out_shape=[jax.ShapeDtypeStruct((b, t, BRANCH_W), F32)] * 2,
        scratch_shapes=[pltpu.VMEM((tt + CONV_PAD, BRANCH_W), F32)],
        compiler_params=_cparams("parallel", "arbitrary"),
        name="lru_gates",
    )(proj, buf_pad, lw["conv_w"], lw["conv_b"], lw["w_a"], lw["b_a"], lw["w_x"], lw["b_x"], lw["lam"])


def _gelu_tanh(x):
    return 0.5 * x * (1.0 + jnp.tanh(math.sqrt(2.0 / math.pi) * (x + 0.044715 * (x * x * x))))


def _lru_scan_kernel(a_ref, u_ref, g_ref, h0_ref, o_ref, ht_ref, h_scr, *, tt):
    ti = pl.program_id(1)

    @pl.when(ti == 0)
    def _():
        h_scr[...] = h0_ref[0]

    def body(i, h):
        h = a_ref[0, i] * h + u_ref[0, i]
        o_ref[0, i] = h
        return h

    h = lax.fori_loop(0, tt, body, h_scr[...], unroll=8)
    h_scr[...] = h
    ht_ref[0] = h
    o_ref[0] = o_ref[0] * _gelu_tanh(g_ref[0])


def _lru_scan(a, u, proj, h0, tt):
    b, t, _ = a.shape
    sub = BRANCH_W // HEAD_DIM
    a4 = a.reshape(b, t, sub, HEAD_DIM)
    u4 = u.reshape(b, t, sub, HEAD_DIM)
    p4 = proj.reshape(b, t, proj.shape[2] // HEAD_DIM, HEAD_DIM)
    blk = pl.BlockSpec((1, tt, sub, HEAD_DIM), lambda i, j: (i, j, 0, 0))
    out, h_t = pl.pallas_call(
        functools.partial(_lru_scan_kernel, tt=tt),
        grid=(b, t // tt),
        in_specs=[
            blk, blk,
            pl.BlockSpec((1, tt, sub, HEAD_DIM), lambda i, j: (i, j, C_LG, 0)),
            pl.BlockSpec((1, sub, HEAD_DIM), lambda i, j: (i, 0, 0)),
        ],
        out_specs=[blk, pl.BlockSpec((1, sub, HEAD_DIM), lambda i, j: (i, 0, 0))],
        out_shape=[
            jax.ShapeDtypeStruct((b, t, sub, HEAD_DIM), F32),
            jax.ShapeDtypeStruct((b, sub, HEAD_DIM), F32),
        ],
        scratch_shapes=[pltpu.VMEM((sub, HEAD_DIM), F32)],
        compiler_params=_cparams("parallel", "arbitrary"),
        name="lru_scan",
    )(a4, u4, p4, h0.reshape(b, sub, HEAD_DIM))
    return out.reshape(b, t, BRANCH_W), h_t.reshape(b, BRANCH_W)


def _trunk_layer(x, mod, w, layer, lam_init, state, past, cfg):
    b, t, d = x.shape
    m = b * t
    shift1, scale1, gate1, shift2, scale2, gate2 = mod
    xm = x.reshape(cfg["mod_shape"])

    u = _normmod(xm, w["norm1_g"], layer, scale1, shift1, cfg["tt_norm"]).reshape(m, d)
    proj2 = _matmul(u, w["w_in"], layer, cfg["tm"], cfg["tn_in"], d)
    proj = proj2.reshape(b, t, proj2.shape[1])

    ret_s0, lru_h0, conv0 = state
    o_ret, ret_s = _retention(proj, ret_s0, w["ret_gn_g"], layer, cfg["ret_chunk"])

    tt = cfg["tt_head"]
    dq = _headnorm(proj, C_DQ, w["diff_qn_g"], layer, DIFF_DQK, tt)
    dk = _headnorm(proj, C_DK, w["diff_kn_g"], layer, DIFF_DQK, tt)
    mq = _headnorm(proj, C_MQ, w["moba_qn_g"], layer, HEAD_DIM, tt)
    mk = _headnorm(proj, C_MK, w["moba_kn_g"], layer, HEAD_DIM, tt)
    if past is None:
        o_dif = _diff_attn(dq, dk, proj, w["diff_lam"], w["diff_subln_g"], layer, lam_init, cfg["tq_diff"])
        o_moba = _moba_attn(mq, mk, proj)
    else:
        cdk, cdv, cmk, cmv, page_table = past
        o_dif = _diff_decode(dq, dk, proj, cdk, cdv, page_table, w["diff_lam"], w["diff_subln_g"], layer, lam_init)
        o_moba = _moba_decode(mq, mk, proj, cmk, cmv, page_table, layer)

    buf_pad = jnp.pad(conv0, ((0, 0), (CONV_PAD - (CONV_W - 1), 0), (0, 0)))
    a, uu = _lru_gates(proj, buf_pad, w["lru"], layer, cfg["tt_lru"])
    o_lru, lru_h = _lru_scan(a, uu, proj, lru_h0, cfg["tt_lru"])

    branch_outs = [o.reshape(m, BRANCH_W) for o in (o_ret, o_dif, o_moba, o_lru)]
    merged = _merge(branch_outs, proj2, w["w_br"], layer, cfg["tm_merge"], cfg["tn_merge"])
    x2 = _matmul_resid(merged, w["w_o"], layer, x.reshape(m, d), gate1, cfg["rows_per_gate"],
                       cfg["tm"], cfg["tn_out"], d)
    u2 = _normmod(x2.reshape(cfg["mod_shape"]), w["norm2_g"], layer, scale2, shift2, cfg["tt_norm"]).reshape(m, d)
    act = _matmul_swiglu(u2, w["w_ffn_in"], layer, cfg["tm"], cfg["tn_ffn"])
    x3 = _matmul_resid(act, w["w_ffn_out"], layer, x2, gate2, cfg["rows_per_gate"],
                       cfg["tm"], cfg["tn_out"], cfg["tk_ffn"])

    lx = proj[:, :, C_LX * BRANCH_W:(C_LX + 1) * BRANCH_W]
    lru_conv = jnp.concatenate([conv0, lx], axis=1)[:, -(CONV_W - 1):]
    new_state = (
        dk.reshape(b, t, N_HEADS, HEAD_DIM),
        proj[:, :, C_DV * BRANCH_W:(C_DV + 1) * BRANCH_W].reshape(b, t, N_HEADS, HEAD_DIM),
        mk.reshape(b, t, N_HEADS, HEAD_DIM),
        proj[:, :, C_MV * BRANCH_W:(C_MV + 1) * BRANCH_W].reshape(b, t, N_HEADS, HEAD_DIM),
        ret_s, lru_h, lru_conv,
    )
    return x3.reshape(b, t, d), new_state


def _prompt_cfg(b, t, d, d_ff):
    return dict(
        mod_shape=(b, t, d), tt_norm=256, tm=1024, tn_in=1024, tn_out=512, tn_ffn=256, tk_ffn=d_ff // 2,
        ret_chunk=128 if t % 128 == 0 else t, tt_head=512, tq_diff=256, tt_lru=512,
        tm_merge=512, tn_merge=512, rows_per_gate=t,
    )


def _sample_cfg(b, t, d, d_ff):
    m = b * t
    return dict(
        mod_shape=(1, m, d), tt_norm=m, tm=m, tn_in=1024, tn_out=1024, tn_ffn=256, tk_ffn=d_ff // 2,
        ret_chunk=128 if t % 128 == 0 else t, tt_head=t, tt_lru=t,
        tm_merge=m, tn_merge=1024, rows_per_gate=m,
    )


def kernel(x_prompt, x_sample, c_prompt, c_sample, cache_dif_k, cache_dif_v, cache_moba_k, cache_moba_v, state_ret, state_lru_h, state_lru_conv, page_table, norm1_g, norm2_g, w_ada, b_ada, w_in, ret_gn_g, diff_qn_g, diff_kn_g, diff_lam, diff_subln_g, moba_qn_g, moba_kn_g, lru_conv_w, lru_conv_b, lru_w_a, lru_b_a, lru_w_x, lru_b_x, lru_lam, w_br, w_o, w_ffn_in, w_ffn_out):
    depth, d = norm1_g.shape
    bp, tp, _ = x_prompt.shape
    bs, ts, _ = x_sample.shape
    d_ff = w_ffn_out.shape[1]
    n_pool = cache_dif_k.shape[1]

    def vec(a):
        return a.reshape(depth, 1, a.shape[-1])

    w = dict(
        norm1_g=vec(norm1_g), norm2_g=vec(norm2_g),
        w_in=w_in.astype(BF16), w_br=w_br.astype(BF16), w_o=w_o.astype(BF16),
        w_ffn_in=w_ffn_in.astype(BF16), w_ffn_out=w_ffn_out.astype(BF16),
        ret_gn_g=vec(ret_gn_g),
        diff_qn_g=vec(jnp.tile(diff_qn_g, (1, 2))), diff_kn_g=vec(jnp.tile(diff_kn_g, (1, 2))),
        diff_lam=diff_lam, diff_subln_g=vec(diff_subln_g),
        moba_qn_g=vec(moba_qn_g), moba_kn_g=vec(moba_kn_g),
        lru=dict(conv_w=lru_conv_w, conv_b=vec(lru_conv_b), w_a=lru_w_a.astype(BF16), b_a=vec(lru_b_a),
                 w_x=lru_w_x.astype(BF16), b_x=vec(lru_b_x), lam=vec(lru_lam)),
    )

    n_c = bp + bs
    rows = -(-n_c // 8) * 8
    c_all = jnp.pad(jnp.concatenate([c_prompt, c_sample], axis=0), ((0, rows - n_c), (0, 0)))
    mod_all = _ada_mod(c_all, w_ada, b_ada).reshape(depth, rows, N_ADA, d)

    flat = lambda a: a.reshape(a.shape[0], a.shape[1], PAGE, BRANCH_W)
    cdk, cdv, cmk, cmv = flat(cache_dif_k), flat(cache_dif_v), flat(cache_moba_k), flat(cache_moba_v)

    cfg_p = _prompt_cfg(bp, tp, d, d_ff)
    cfg_s = _sample_cfg(bs, ts, d, d_ff)
    zero_state = (jnp.zeros((bp, N_HEADS, HEAD_DIM, HEAD_DIM), F32), jnp.zeros((bp, BRANCH_W), F32),
                  jnp.zeros((bp, CONV_W - 1, BRANCH_W), F32))

    y_p, y_s = x_prompt, x_sample
    new_p = [[] for _ in range(7)]
    new_s = [[] for _ in range(7)]
    for l in range(depth):
        lam_init = 0.8 - 0.6 * math.exp(-0.3 * l)
        mod_p = [mod_all[l, :bp, i][:, None, :] for i in range(N_ADA)]
        mod_s = [jnp.repeat(mod_all[l, bp:n_c, i], ts, axis=0)[None] for i in range(N_ADA)]
        y_p, st_p = _trunk_layer(y_p, mod_p, w, l, lam_init, zero_state, None, cfg_p)
        y_s, st_s = _trunk_layer(y_s, mod_s, w, l, lam_init,
                                 (state_ret[l], state_lru_h[l], state_lru_conv[l]),
                                 (cdk, cdv, cmk, cmv, page_table), cfg_s)
        for acc, a in zip(new_p, st_p):
            acc.append(a)
        for acc, a in zip(new_s, st_s):
            acc.append(a)
    outs_p = [jnp.stack(a) for a in new_p]
    outs_s = [jnp.stack(a) for a in new_s]
    return (y_p, y_s, *outs_p, *outs_s)
```

```python
import functools
import math

import numpy as np
import jax
import jax.numpy as jnp
from jax import lax
from jax.experimental import pallas as pl
from jax.experimental.pallas import tpu as pltpu

F32 = jnp.float32
BF16 = jnp.bfloat16

HEAD_DIM = 128
N_HEADS = 8
BRANCH_W = N_HEADS * HEAD_DIM
N_BRANCH = 4
DIFF_DQK = HEAD_DIM // 2
MOBA_BLOCK = 256
MOBA_TOPK = 3
CONV_W = 4
LRU_C = 8.0
N_ADA = 6
EPS = 1e-6
NEG_INF = -1e30
BELOW_NEG_INF = -3e38
PAGE = 128
VMEM_LIMIT = 56 * 1024 * 1024

C_RQ, C_RK, C_RV, C_RG, C_DQ, C_DK, C_DV, C_MQ, C_MK, C_MV, C_LX, C_LG, C_GATES = range(13)


def _cparams(*sem):
    return pltpu.CompilerParams(dimension_semantics=sem, vmem_limit_bytes=VMEM_LIMIT)


def _sigmoid(x):
    return 1.0 / (1.0 + jnp.exp(-x))


def _dot(a, b):
    return jnp.dot(a, b, preferred_element_type=F32)


def _dot_nt(a, b):
    return lax.dot_general(a, b, (((1,), (1,)), ((), ())), preferred_element_type=F32)


def _dot_tn(a, b):
    return lax.dot_general(a, b, (((0,), (0,)), ((), ())), preferred_element_type=F32)


def _ada_kernel(c_ref, w_ref, b_ref, o_ref):
    c = c_ref[...]
    a = (c * _sigmoid(c)).astype(BF16)
    o_ref[0] = _dot(a, w_ref[0].astype(BF16)) + b_ref[0]


def _ada_mod(c_all, w_ada, b_ada):
    depth, d, n = w_ada.shape
    r = c_all.shape[0]
    tn = 1024
    return pl.pallas_call(
        _ada_kernel,
        grid=(depth, n // tn),
        in_specs=[
            pl.BlockSpec((r, d), lambda l, j: (0, 0)),
            pl.BlockSpec((1, d, tn), lambda l, j: (l, 0, j)),
            pl.BlockSpec((1, 1, tn), lambda l, j: (l, 0, j)),
        ],
        out_specs=pl.BlockSpec((1, r, tn), lambda l, j: (l, 0, j)),
        out_shape=jax.ShapeDtypeStruct((depth, r, n), F32),
        compiler_params=_cparams("parallel", "parallel"),
        name="ada_mod",
    )(c_all, w_ada, b_ada.reshape(depth, 1, n))


def _normmod_kernel(x_ref, g_ref, sc_ref, sh_ref, o_ref):
    x = x_ref[0]
    y = x * lax.rsqrt(jnp.mean(x * x, axis=-1, keepdims=True) + EPS)
    y = y * g_ref[0]
    o_ref[0] = (y * (1.0 + sc_ref[0]) + sh_ref[0]).astype(o_ref.dtype)


def _normmod(x, g_all, layer, scale, shift, tt):
    b, t, d = x.shape
    r = scale.shape[1]
    rb = 1 if r == 1 else tt
    mod_map = (lambda i, j: (i, 0, 0)) if r == 1 else (lambda i, j: (i, j, 0))
    return pl.pallas_call(
        _normmod_kernel,
        grid=(b, t // tt),
        in_specs=[
            pl.BlockSpec((1, tt, d), lambda i, j: (i, j, 0)),
            pl.BlockSpec((1, 1, d), lambda i, j: (layer, 0, 0)),
            pl.BlockSpec((1, rb, d), mod_map),
            pl.BlockSpec((1, rb, d), mod_map),
        ],
        out_specs=pl.BlockSpec((1, tt, d), lambda i, j: (i, j, 0)),
        out_shape=jax.ShapeDtypeStruct((b, t, d), BF16),
        compiler_params=_cparams("parallel", "parallel"),
        name="norm_mod",
    )(x, g_all, scale, shift)


def _stage_weight(b_ref, wb_s):
    @pl.when(pl.program_id(1) == 0)
    def _():
        wb_s[...] = b_ref[0].astype(BF16)


def _mm_kernel(a_ref, b_ref, o_ref, wb_s):
    _stage_weight(b_ref, wb_s)
    o_ref[...] = _dot(a_ref[...], wb_s[...]).astype(o_ref.dtype)


def _matmul(a, w_all, layer, tm, tn):
    m, kk = a.shape
    n = w_all.shape[2]
    return pl.pallas_call(
        _mm_kernel,
        grid=(n // tn, m // tm),
        in_specs=[
            pl.BlockSpec((tm, kk), lambda j, i: (i, 0)),
            pl.BlockSpec((1, kk, tn), lambda j, i: (layer, 0, j)),
        ],
        out_specs=pl.BlockSpec((tm, tn), lambda j, i: (i, j)),
        out_shape=jax.ShapeDtypeStruct((m, n), F32),
        scratch_shapes=[pltpu.VMEM((kk, tn), BF16)],
        compiler_params=_cparams("parallel", "arbitrary"),
        name="matmul",
    )(a, w_all)


def _mm_resid_kernel(a_ref, b_ref, x_ref, g_ref, o_ref, wb_s):
    _stage_weight(b_ref, wb_s)
    o_ref[...] = x_ref[...] + g_ref[0] * _dot(a_ref[...], wb_s[...])


def _matmul_resid(a, w_all, layer, x, gate, rows_per_gate, tm, tn):
    m, kk = a.shape
    n = w_all.shape[2]
    r = gate.shape[1]
    tiles_per_gate = max(rows_per_gate // tm, 1)
    return pl.pallas_call(
        _mm_resid_kernel,
        grid=(n // tn, m // tm),
        in_specs=[
            pl.BlockSpec((tm, kk), lambda j, i: (i, 0)),
            pl.BlockSpec((1, kk, tn), lambda j, i: (layer, 0, j)),
            pl.BlockSpec((tm, tn), lambda j, i: (i, j)),
            pl.BlockSpec((1, r, tn), lambda j, i: (i // tiles_per_gate, 0, j)),
        ],
        out_specs=pl.BlockSpec((tm, tn), lambda j, i: (i, j)),
        out_shape=jax.ShapeDtypeStruct((m, n), F32),
        scratch_shapes=[pltpu.VMEM((kk, tn), BF16)],
        compiler_params=_cparams("parallel", "arbitrary"),
        name="matmul_resid",
    )(a, w_all, x, gate)


def _mm_resid_ksplit_kernel(a_ref, b_ref, x_ref, g_ref, o_ref, acc_ref, *, nk):
    k = pl.program_id(2)
    d = _dot(a_ref[...], b_ref[0])

    def finish(acc):
        o_ref[...] = x_ref[...] + g_ref[0] * acc

    if nk == 1:
        finish(d)
        return

    @pl.when(k == 0)
    def _():
        acc_ref[...] = d

    @pl.when(k > 0)
    def _():
        acc_ref[...] += d

    @pl.when(k == nk - 1)
    def _():
        finish(acc_ref[...])


def _matmul_resid_ksplit(a, w_all, layer, x, gate, rows_per_gate, tm, tn, tk):
    m, kk = a.shape
    n = w_all.shape[2]
    nk = kk // tk
    r = gate.shape[1]
    tiles_per_gate = max(rows_per_gate // tm, 1)
    return pl.pallas_call(
        functools.partial(_mm_resid_ksplit_kernel, nk=nk),
        grid=(m // tm, n // tn, nk),
        in_specs=[
            pl.BlockSpec((tm, tk), lambda i, j, k: (i, k)),
            pl.BlockSpec((1, tk, tn), lambda i, j, k: (layer, k, j)),
            pl.BlockSpec((tm, tn), lambda i, j, k: (i, j)),
            pl.BlockSpec((1, r, tn), lambda i, j, k: (i // tiles_per_gate, 0, j)),
        ],
        out_specs=pl.BlockSpec((tm, tn), lambda i, j, k: (i, j)),
        out_shape=jax.ShapeDtypeStruct((m, n), F32),
        scratch_shapes=[pltpu.VMEM((tm, tn) if nk > 1 else (8, 128), F32)],
        compiler_params=_cparams("parallel", "parallel", "arbitrary"),
        name="matmul_resid",
    )(a, w_all, x, gate)


def _mm_swiglu_kernel(a_ref, bg_ref, bu_ref, o_ref, wg_s, wu_s):
    _stage_weight(bg_ref, wg_s)
    _stage_weight(bu_ref, wu_s)
    a = a_ref[...]
    hg = _dot(a, wg_s[...])
    hu = _dot(a, wu_s[...])
    o_ref[...] = (hg * _sigmoid(hg) * hu).astype(o_ref.dtype)


def _matmul_swiglu(a, w_all, layer, tm, tn):
    m, kk = a.shape
    f = w_all.shape[2] // 2
    nj = f // tn
    panel = pltpu.VMEM((kk, tn), BF16)
    return pl.pallas_call(
        _mm_swiglu_kernel,
        grid=(nj, m // tm),
        in_specs=[
            pl.BlockSpec((tm, kk), lambda j, i: (i, 0)),
            pl.BlockSpec((1, kk, tn), lambda j, i: (layer, 0, j)),
            pl.BlockSpec((1, kk, tn), lambda j, i: (layer, 0, j + nj)),
        ],
        out_specs=pl.BlockSpec((tm, tn), lambda j, i: (i, j)),
        out_shape=jax.ShapeDtypeStruct((m, f), BF16),
        scratch_shapes=[panel, panel],
        compiler_params=_cparams("parallel", "arbitrary"),
        name="matmul_swiglu",
    )(a, w_all, w_all)


def _merge_kernel(o0, o1, o2, o3, g0, g1, g2, g3, w_ref, out_ref, wb_s):
    _stage_weight(w_ref, wb_s)
    acc = None
    for i, (o, g) in enumerate(((o0, g0), (o1, g1), (o2, g2), (o3, g3))):
        t = _sigmoid(g[...]) * _dot(o[...].astype(BF16), wb_s[i])
        acc = t if acc is None else acc + t
    out_ref[...] = acc.astype(out_ref.dtype)


def _merge(branch_outs, proj, w_br_all, layer, tm, tn):
    m = proj.shape[0]
    d = w_br_all.shape[3]
    gate_blk0 = C_GATES * BRANCH_W // tn
    per_branch = d // tn
    o_specs = [pl.BlockSpec((tm, BRANCH_W), lambda j, i: (i, 0)) for _ in range(N_BRANCH)]
    g_specs = [
        pl.BlockSpec((tm, tn), functools.partial(lambda j, i, br: (i, gate_blk0 + br * per_branch + j), br=br))
        for br in range(N_BRANCH)
    ]
    return pl.pallas_call(
        _merge_kernel,
        grid=(d // tn, m // tm),
        in_specs=o_specs + g_specs + [pl.BlockSpec((1, N_BRANCH, BRANCH_W, tn), lambda j, i: (layer, 0, 0, j))],
        out_specs=pl.BlockSpec((tm, tn), lambda j, i: (i, j)),
        out_shape=jax.ShapeDtypeStruct((m, d), BF16),
        scratch_shapes=[pltpu.VMEM((N_BRANCH, BRANCH_W, tn), BF16)],
        compiler_params=_cparams("parallel", "arbitrary"),
        name="merge",
    )(*branch_outs, proj, proj, proj, proj, w_br_all)


def _headnorm_kernel(x_ref, g_ref, o_ref, *, group):
    x = x_ref[0]
    tt = x.shape[0]
    g = g_ref[0]
    for h in range(N_HEADS):
        xh = x[:, h * HEAD_DIM:(h + 1) * HEAD_DIM]
        x2 = xh * xh
        if group == HEAD_DIM:
            ms = jnp.mean(x2, axis=-1, keepdims=True)
        else:
            lo = lax.broadcasted_iota(jnp.int32, (tt, HEAD_DIM), 1) < group
            s_lo = jnp.sum(jnp.where(lo, x2, 0.0), axis=-1, keepdims=True)
            s_hi = jnp.sum(jnp.where(lo, 0.0, x2), axis=-1, keepdims=True)
            ms = jnp.where(lo, s_lo, s_hi) * (1.0 / group)
        o_ref[0, :, h * HEAD_DIM:(h + 1) * HEAD_DIM] = xh * lax.rsqrt(ms + EPS) * g


def _headnorm(proj, col, g_all, layer, group, tt):
    b, t, _ = proj.shape
    return pl.pallas_call(
        functools.partial(_headnorm_kernel, group=group),
        grid=(b, t // tt),
        in_specs=[
            pl.BlockSpec((1, tt, BRANCH_W), lambda i, j: (i, j, col)),
            pl.BlockSpec((1, 1, HEAD_DIM), lambda i, j: (layer, 0, 0)),
        ],
        out_specs=pl.BlockSpec((1, tt, BRANCH_W), lambda i, j: (i, j, 0)),
        out_shape=jax.ShapeDtypeStruct((b, t, BRANCH_W), F32),
        compiler_params=_cparams("parallel", "parallel"),
        name="head_norm",
    )(proj, g_all)


def _ret_kernel(q_ref, k_ref, v_ref, g_ref, s0_ref, gn_ref, o_ref, s_out_ref, s_scr, *, c, nc):
    ci = pl.program_id(1)

    @pl.when(ci == 0)
    def _():
        s_scr[...] = s0_ref[0]

    ii = lax.broadcasted_iota(jnp.int32, (c, c), 0)
    jj = lax.broadcasted_iota(jnp.int32, (c, c), 1)
    dist = (ii - jj).astype(F32)
    idx = lax.broadcasted_iota(jnp.int32, (c, 1), 0).astype(F32)
    log_decay = np.log(1.0 - 2.0 ** (-5.0 - np.arange(N_HEADS)))
    for h in range(N_HEADS):
        lg = float(np.float32(log_decay[h]))
        sl = slice(h * HEAD_DIM, (h + 1) * HEAD_DIM)
        decay = jnp.where(dist >= 0, jnp.exp(lg * jnp.maximum(dist, 0.0)), 0.0)
        q_decay = jnp.exp(lg * (idx + 1.0))
        k_decay = jnp.exp(lg * (c - 1.0 - idx))
        s_decay = math.exp(lg * c)

        q = q_ref[0, :, sl]
        k = k_ref[0, :, sl] * HEAD_DIM ** -0.5
        qb = q.astype(BF16)
        vb = v_ref[0, :, sl].astype(BF16)
        s = s_scr[h]
        inner = _dot_nt(qb, k.astype(BF16)) * decay
        o = _dot(inner.astype(BF16), vb) + _dot(qb, s.astype(BF16)) * q_decay
        s_scr[h] = s * s_decay + _dot_tn((k * k_decay).astype(BF16), vb)

        mu = jnp.mean(o, axis=-1, keepdims=True)
        var = jnp.mean(jnp.square(o - mu), axis=-1, keepdims=True)
        o = (o - mu) * lax.rsqrt(var + EPS) * gn_ref[0]
        g = g_ref[0, :, sl]
        o_ref[0, :, sl] = (o * (g * _sigmoid(g))).astype(o_ref.dtype)

    @pl.when(ci == nc - 1)
    def _():
        s_out_ref[0] = s_scr[...]


def _retention(proj, s0, gn_all, layer, c):
    b, t, _ = proj.shape
    nc = t // c

    def col_spec(col):
        return pl.BlockSpec((1, c, BRANCH_W), lambda i, j: (i, j, col))

    state_spec = pl.BlockSpec((1, N_HEADS, HEAD_DIM, HEAD_DIM), lambda i, j: (i, 0, 0, 0))
    return pl.pallas_call(
        functools.partial(_ret_kernel, c=c, nc=nc),
        grid=(b, nc),
        in_specs=[
            col_spec(C_RQ), col_spec(C_RK), col_spec(C_RV), col_spec(C_RG),
            state_spec,
            pl.BlockSpec((1, 1, HEAD_DIM), lambda i, j: (layer, 0, 0)),
        ],
        out_specs=[pl.BlockSpec((1, c, BRANCH_W), lambda i, j: (i, j, 0)), state_spec],
        out_shape=[
            jax.ShapeDtypeStruct((b, t, BRANCH_W), BF16 if c % 16 == 0 else F32),
            jax.ShapeDtypeStruct((b, N_HEADS, HEAD_DIM, HEAD_DIM), F32),
        ],
        scratch_shapes=[pltpu.VMEM((N_HEADS, HEAD_DIM, HEAD_DIM), F32)],
        compiler_params=_cparams("parallel", "arbitrary"),
        name="retention",
    )(proj, proj, proj, proj, s0, gn_all)


def _alibi_slopes():
    return 2.0 ** (-8.0 * np.arange(1, N_HEADS + 1) / N_HEADS)


def _diff_lambda(lam_ref, lam_init):
    lp = lam_ref[0]
    s01 = jnp.sum(lp[0:1] * lp[1:2], axis=-1, keepdims=True)
    s23 = jnp.sum(lp[2:3] * lp[3:4], axis=-1, keepdims=True)
    return jnp.exp(s01) - jnp.exp(s23) + lam_init


def _softmax_update(s, v_bf16, m_ref, l_ref, acc_ref):
    m_old = m_ref[...]
    m_new = jnp.maximum(m_old, jnp.max(s, axis=-1, keepdims=True))
    alpha = jnp.exp(m_old - m_new)
    p = jnp.exp(s - m_new)
    l_ref[...] = alpha * l_ref[...] + jnp.sum(p, axis=-1, keepdims=True)
    acc_ref[...] = alpha * acc_ref[...] + _dot(p.astype(BF16), v_bf16)
    m_ref[...] = m_new


def _subln(o, g, lam_init):
    return o * lax.rsqrt(jnp.mean(o * o, axis=-1, keepdims=True) + EPS) * g * (1.0 - lam_init)


def _stage_kv(k_ref, v_ref, kb_s, vt_s, tk):
    kb_s[...] = k_ref[0].astype(BF16)
    for n in range(kb_s.shape[0] // tk):
        vt_s[n] = v_ref[0, n * tk:(n + 1) * tk, :].T.astype(BF16)


def _softmax_update_t(s, shift, vt_bf16, m_ref, l_ref, acc_ref):
    m_old = m_ref[...]
    m_new = jnp.maximum(m_old, jnp.max(s, axis=0, keepdims=True) - shift)
    alpha = jnp.exp(m_old - m_new)
    p = jnp.exp(s - (m_new + shift))
    l_ref[...] = alpha * l_ref[...] + jnp.sum(p, axis=0, keepdims=True)
    acc_ref[...] = alpha * acc_ref[...] + _dot(vt_bf16, p.astype(BF16))
    m_ref[...] = m_new


def _tile_shift(slope, tiles_apart, tq, width):
    return slope * (tiles_apart * tq + jnp.zeros((1, width), jnp.int32)).astype(F32)


def _diff_attn_kernel(sl_ref, lam_ref, q_ref, k_ref, v_ref, sg_ref, o_ref,
                      kb_s, vt_s, bias_s, m_s, l_s, acc, *, tq, lam_init):
    qi = pl.program_id(2)

    slope = sl_ref[0][:, :1]
    rel = lax.broadcasted_iota(jnp.int32, (tq, tq), 1) - lax.broadcasted_iota(jnp.int32, (tq, tq), 0)
    rel2 = jnp.concatenate([rel, rel], axis=1)

    @pl.when(qi == 0)
    def _():
        _stage_kv(k_ref, v_ref, kb_s, vt_s, tq)
        bias_s[0:tq, :] = slope * rel2.astype(F32)
        bias_s[tq:2 * tq, :] = slope * (rel2 - tq).astype(F32)

    qt = (q_ref[0] * DIFF_DQK ** -0.5).T
    chan = lax.broadcasted_iota(jnp.int32, (HEAD_DIM, tq), 0)
    qcat = jnp.concatenate([jnp.where(chan < DIFF_DQK, qt, 0.0), jnp.where(chan < DIFF_DQK, 0.0, qt)],
                           axis=1).astype(BF16)
    m_s[...] = jnp.full(m_s.shape, NEG_INF, F32)
    l_s[...] = jnp.zeros(l_s.shape, F32)
    acc[...] = jnp.zeros(acc.shape, F32)

    def span(j, nt, masked):
        start = pl.multiple_of(j * tq, tq)
        s = _dot(kb_s[pl.ds(start, nt * tq), :], qcat) - bias_s[0:nt * tq, :]
        if masked:
            s = jnp.where(rel2 >= 0, s, NEG_INF)
        vt = vt_s[j] if nt == 1 else jnp.concatenate([vt_s[j], vt_s[j + 1]], axis=1)
        _softmax_update_t(s, _tile_shift(slope, qi - j, tq, 2 * tq), vt, m_s, l_s, acc)

    span(qi, 1, True)

    def body(jj, carry):
        span(2 * jj, 2, False)
        return carry

    lax.fori_loop(0, qi // 2, body, 0)

    @pl.when(qi % 2 == 1)
    def _():
        span(qi - 1, 1, False)

    lam = _diff_lambda(lam_ref, lam_init)
    o = acc[...] / l_s[...]
    o = (o[:, 0:tq] - lam * o[:, tq:2 * tq]).T
    o_ref[0] = _subln(o, sg_ref[0], lam_init).astype(o_ref.dtype)


def _diff_attn(qn, kn, proj, lam_all, sg_all, layer, lam_init, tq):
    b, t, _ = qn.shape
    sl_tab = jnp.asarray(np.broadcast_to(_alibi_slopes()[:, None, None], (N_HEADS, 1, HEAD_DIM)), dtype=F32)
    stat = pltpu.VMEM((1, 2 * tq), F32)
    return pl.pallas_call(
        functools.partial(_diff_attn_kernel, tq=tq, lam_init=lam_init),
        grid=(b, N_HEADS, t // tq),
        in_specs=[
            pl.BlockSpec((1, 1, HEAD_DIM), lambda i, h, j: (h, 0, 0)),
            pl.BlockSpec((1, 4, DIFF_DQK), lambda i, h, j: (layer, 0, 0)),
            pl.BlockSpec((1, tq, HEAD_DIM), lambda i, h, j: (i, j, h)),
            pl.BlockSpec((1, t, HEAD_DIM), lambda i, h, j: (i, 0, h)),
            pl.BlockSpec((1, t, HEAD_DIM), lambda i, h, j: (i, 0, C_DV * N_HEADS + h)),
            pl.BlockSpec((1, 1, HEAD_DIM), lambda i, h, j: (layer, 0, 0)),
        ],
        out_specs=pl.BlockSpec((1, tq, HEAD_DIM), lambda i, h, j: (i, j, h)),
        out_shape=jax.ShapeDtypeStruct((b, t, BRANCH_W), BF16),
        scratch_shapes=[pltpu.VMEM((t, HEAD_DIM), BF16), pltpu.VMEM((t // tq, HEAD_DIM, tq), BF16),
                        pltpu.VMEM((2 * tq, 2 * tq), F32), stat, stat, pltpu.VMEM((HEAD_DIM, 2 * tq), F32)],
        compiler_params=_cparams("parallel", "parallel", "arbitrary"),
        name="diff_attn",
    )(sl_tab, lam_all, qn, kn, proj, sg_all)


def _split_bf16(x):
    hi = x.astype(BF16)
    lo = (x - hi.astype(F32)).astype(BF16)
    return hi, lo


def _moba_attn_kernel(sl_ref, q_ref, k_ref, v_ref, o_ref, kb_s, vt_s, bias_s, kmean, allowed_s, m_s, l_s, acc, *, nb):
    tq = MOBA_BLOCK
    qi = pl.program_id(2)
    slope = sl_ref[0][:, :1]
    nbp = kmean.shape[0]
    rel = lax.broadcasted_iota(jnp.int32, (tq, tq), 1) - lax.broadcasted_iota(jnp.int32, (tq, tq), 0)

    @pl.when(qi == 0)
    def _():
        _stage_kv(k_ref, v_ref, kb_s, vt_s, tq)
        bias_s[0:tq, :] = slope * rel.astype(F32)
        bias_s[tq:2 * tq, :] = slope * (rel - tq).astype(F32)
        kmean[...] = jnp.zeros(kmean.shape, F32)
        for n in range(nb):
            kmean[n:n + 1, :] = jnp.mean(k_ref[0, n * tq:(n + 1) * tq, :], axis=0, keepdims=True)

    qt = q_ref[0].T
    qb = qt.astype(BF16)
    qh, ql = _split_bf16(qt)
    kh, kl = _split_bf16(kmean[...])
    gate = _dot(kh, qh) + (_dot(kl, qh) + _dot(kh, ql))
    blk = lax.broadcasted_iota(jnp.int32, (nbp, tq), 0)
    g = jnp.where(blk < qi, gate, NEG_INF)
    allowed = jnp.zeros((nbp, tq), F32)
    for _ in range(MOBA_TOPK):
        mx = jnp.max(g, axis=0, keepdims=True)
        first = jnp.min(jnp.where(g == mx, blk, nbp), axis=0, keepdims=True)
        hit = blk == first
        allowed = jnp.where(hit & (mx > 0.5 * NEG_INF), 1.0, allowed)
        g = jnp.where(hit, BELOW_NEG_INF, g)
    allowed_s[...] = allowed

    m_s[...] = jnp.full(m_s.shape, NEG_INF, F32)
    l_s[...] = jnp.zeros(l_s.shape, F32)
    acc[...] = jnp.zeros(acc.shape, F32)
    scale = HEAD_DIM ** -0.5

    def span(j, nt, own):
        start = pl.multiple_of(j * tq, tq)
        s = _dot(kb_s[pl.ds(start, nt * tq), :], qb) * scale - bias_s[0:nt * tq, :]
        if own:
            s = jnp.where(rel >= 0, s, NEG_INF)
        else:
            picked = [jnp.broadcast_to(allowed_s[pl.ds(j + i, 1), :], (tq, tq)) for i in range(nt)]
            s = jnp.where(jnp.concatenate(picked, axis=0) > 0.0, s, NEG_INF)
        vt = vt_s[j] if nt == 1 else jnp.concatenate([vt_s[j], vt_s[j + 1]], axis=1)
        _softmax_update_t(s, _tile_shift(slope, qi - j, tq, tq), vt, m_s, l_s, acc)

    span(qi, 1, True)

    def body(jj, carry):
        span(2 * jj, 2, False)
        return carry

    lax.fori_loop(0, qi // 2, body, 0)

    @pl.when(qi % 2 == 1)
    def _():
        span(qi - 1, 1, False)

    o_ref[0] = (acc[...] / l_s[...]).T.astype(o_ref.dtype)


def _moba_attn(qn, kn, proj):
    b, t, _ = qn.shape
    assert t % MOBA_BLOCK == 0
    nb = t // MOBA_BLOCK
    nbp = -(-nb // 8) * 8
    sl_tab = jnp.asarray(np.broadcast_to(_alibi_slopes()[:, None, None], (N_HEADS, 1, HEAD_DIM)), dtype=F32)
    return pl.pallas_call(
        functools.partial(_moba_attn_kernel, nb=nb),
        grid=(b, N_HEADS, nb),
        in_specs=[
            pl.BlockSpec((1, 1, HEAD_DIM), lambda i, h, j: (h, 0, 0)),
            pl.BlockSpec((1, MOBA_BLOCK, HEAD_DIM), lambda i, h, j: (i, j, h)),
            pl.BlockSpec((1, t, HEAD_DIM), lambda i, h, j: (i, 0, h)),
            pl.BlockSpec((1, t, HEAD_DIM), lambda i, h, j: (i, 0, C_MV * N_HEADS + h)),
        ],
        out_specs=pl.BlockSpec((1, MOBA_BLOCK, HEAD_DIM), lambda i, h, j: (i, j, h)),
        out_shape=jax.ShapeDtypeStruct((b, t, BRANCH_W), BF16),
        scratch_shapes=[
            pltpu.VMEM((t, HEAD_DIM), BF16),
            pltpu.VMEM((nb, HEAD_DIM, MOBA_BLOCK), BF16),
            pltpu.VMEM((2 * MOBA_BLOCK, MOBA_BLOCK), F32),
            pltpu.VMEM((nbp, HEAD_DIM), F32),
            pltpu.VMEM((nbp, MOBA_BLOCK), F32),
            pltpu.VMEM((1, MOBA_BLOCK), F32),
            pltpu.VMEM((1, MOBA_BLOCK), F32),
            pltpu.VMEM((HEAD_DIM, MOBA_BLOCK), F32),
        ],
        compiler_params=_cparams("parallel", "parallel", "arbitrary"),
        name="moba_attn",
    )(sl_tab, qn, kn, proj)


def _query_rows(q, maps):
    t = q.shape[0]
    rows = N_HEADS * maps * t
    width = HEAD_DIM // maps
    tiled = jnp.concatenate([q] * (N_HEADS * maps), axis=0)
    r = lax.broadcasted_iota(jnp.int32, (rows, BRANCH_W), 0)
    ch = lax.broadcasted_iota(jnp.int32, (rows, BRANCH_W), 1)
    return jnp.where(r // t == ch // width, tiled, 0.0)


PAGES_PER_STEP = 4


def _page_specs(layer, n_pages, pps):
    def spec(k):
        return pl.BlockSpec((1, 1, PAGE, BRANCH_W),
                            lambda i, p, pt: (layer, pt[i * n_pages + p * pps + k], 0, 0))
    return [spec(k) for k in range(pps)]


def _diff_dec_kernel(pt_ref, sl_ref, lam_ref, q_ref, kn_ref, vn_ref, sg_ref, *rest, t, n_pages, pps, lam_init):
    kc_refs, vc_refs = rest[:pps], rest[pps:2 * pps]
    o_ref, qrows, m_s, l_s, acc = rest[2 * pps:]
    p = pl.program_id(1)
    rows = N_HEADS * 2 * t
    p_len = n_pages * PAGE
    span = pps * PAGE

    @pl.when(p == 0)
    def _():
        qrows[...] = _query_rows(q_ref[0] * DIFF_DQK ** -0.5, 2).astype(BF16)
        m_s[...] = jnp.full(m_s.shape, NEG_INF, F32)
        l_s[...] = jnp.zeros(l_s.shape, F32)
        acc[...] = jnp.zeros(acc.shape, F32)

    slope = sl_ref[:, :1]
    qpos_w = p_len + lax.broadcasted_iota(jnp.int32, (rows, span), 0) % t
    tok_w = lax.broadcasted_iota(jnp.int32, (rows, span), 1)
    dist = (qpos_w - (p * span + tok_w)).astype(F32)
    kb = jnp.concatenate([r[0, 0] for r in kc_refs], axis=0).astype(BF16)
    vb = jnp.concatenate([r[0, 0] for r in vc_refs], axis=0).astype(BF16)
    s = _dot_nt(qrows[...], kb) - slope * dist
    _softmax_update(s, vb, m_s, l_s, acc)

    @pl.when(p == n_pages // pps - 1)
    def _():
        qpos = p_len + lax.broadcasted_iota(jnp.int32, (rows, PAGE), 0) % t
        tok = lax.broadcasted_iota(jnp.int32, (rows, PAGE), 1)
        pad = jnp.zeros((PAGE - t, BRANCH_W), F32)
        kn = jnp.concatenate([kn_ref[0], pad], axis=0).astype(BF16)
        vn = jnp.concatenate([vn_ref[0], pad], axis=0).astype(BF16)
        dist_n = (qpos - (p_len + tok)).astype(F32)
        s_n = _dot_nt(qrows[...], kn) - slope * dist_n
        s_n = jnp.where((dist_n >= 0) & (tok < t), s_n, NEG_INF)
        _softmax_update(s_n, vn, m_s, l_s, acc)
        lam = _diff_lambda(lam_ref, lam_init)
        out = acc[...] / l_s[...]
        for h in range(N_HEADS):
            blk = out[h * 2 * t:(h + 1) * 2 * t, h * HEAD_DIM:(h + 1) * HEAD_DIM]
            o = blk[0:t] - lam * blk[t:2 * t]
            o_ref[0, :, h * HEAD_DIM:(h + 1) * HEAD_DIM] = _subln(o, sg_ref[0], lam_init)


def _diff_decode(qn, kn, proj, cache_k, cache_v, page_table, lam_all, sg_all, layer, lam_init):
    b, t, _ = qn.shape
    n_pages = page_table.shape[1]
    pps = PAGES_PER_STEP
    assert n_pages % pps == 0
    rows = N_HEADS * 2 * t
    slopes = np.repeat(_alibi_slopes(), 2 * t)
    sl_tab = jnp.asarray(np.broadcast_to(slopes[:, None], (rows, HEAD_DIM)), dtype=F32)
    grid_spec = pltpu.PrefetchScalarGridSpec(
        num_scalar_prefetch=1,
        grid=(b, n_pages // pps),
        in_specs=[
            pl.BlockSpec((rows, HEAD_DIM), lambda i, p, pt: (0, 0)),
            pl.BlockSpec((1, 4, DIFF_DQK), lambda i, p, pt: (layer, 0, 0)),
            pl.BlockSpec((1, t, BRANCH_W), lambda i, p, pt: (i, 0, 0)),
            pl.BlockSpec((1, t, BRANCH_W), lambda i, p, pt: (i, 0, 0)),
            pl.BlockSpec((1, t, BRANCH_W), lambda i, p, pt: (i, 0, C_DV)),
            pl.BlockSpec((1, 1, HEAD_DIM), lambda i, p, pt: (layer, 0, 0)),
        ] + _page_specs(layer, n_pages, pps) + _page_specs(layer, n_pages, pps),
        out_specs=pl.BlockSpec((1, t, BRANCH_W), lambda i, p, pt: (i, 0, 0)),
        scratch_shapes=[
            pltpu.VMEM((rows, BRANCH_W), BF16),
            pltpu.VMEM((rows, 1), F32),
            pltpu.VMEM((rows, 1), F32),
            pltpu.VMEM((rows, BRANCH_W), F32),
        ],
    )
    return pl.pallas_call(
        functools.partial(_diff_dec_kernel, t=t, n_pages=n_pages, pps=pps, lam_init=lam_init),
        grid_spec=grid_spec,
        out_shape=jax.ShapeDtypeStruct((b, t, BRANCH_W), F32),
        compiler_params=_cparams("parallel", "arbitrary"),
        name="diff_decode",
    )(page_table.reshape(-1), sl_tab, lam_all, qn, kn, proj, sg_all, *([cache_k] * pps), *([cache_v] * pps))


def _diag_blocks(x, t):
    return jnp.concatenate(
        [x[h * t:(h + 1) * t, h * HEAD_DIM:(h + 1) * HEAD_DIM] for h in range(N_HEADS)], axis=0)


def _moba_dec_kernel(pt_ref, sl_ref, q_ref, kn_ref, vn_ref, *rest, t, n_pages, pps):
    kc_refs, vc_refs = rest[:pps], rest[pps:2 * pps]
    o_ref, qrows, gsum, m_pg, l_pg, acc_pg = rest[2 * pps:]
    p = pl.program_id(1)
    rows = N_HEADS * t
    p_len = n_pages * PAGE
    n_blk = n_pages // 2
    scale = HEAD_DIM ** -0.5
    wide = (rows, HEAD_DIM)

    @pl.when(p == 0)
    def _():
        qrows[...] = _query_rows(q_ref[0], 1).astype(BF16)

    slope = sl_ref[:, :1]
    qpos = p_len + lax.broadcasted_iota(jnp.int32, (rows, PAGE), 0) % t
    tok = lax.broadcasted_iota(jnp.int32, (rows, PAGE), 1)

    def partial_softmax(raw, dist, mask, v_bf16):
        s = raw * scale - slope * dist
        if mask is not None:
            s = jnp.where(mask, s, NEG_INF)
        m = jnp.max(s, axis=-1, keepdims=True)
        e = jnp.exp(s - m)
        if mask is not None:
            e = jnp.where(mask, e, 0.0)
        l = jnp.sum(e, axis=-1, keepdims=True)
        pv = _diag_blocks(_dot(e.astype(BF16), v_bf16), t)
        return m, l, pv

    for k in range(pps):
        pg = p * pps + k
        raw = _dot_nt(qrows[...], kc_refs[k][0, 0].astype(BF16))
        dist = (qpos - (pg * PAGE + tok)).astype(F32)
        m, l, pv = partial_softmax(raw, dist, None, vc_refs[k][0, 0].astype(BF16))
        gsum[pg] = jnp.broadcast_to(jnp.sum(raw, axis=-1, keepdims=True), wide)
        m_pg[pg] = jnp.broadcast_to(m, wide)
        l_pg[pg] = jnp.broadcast_to(l, wide)
        acc_pg[pg] = pv

    @pl.when(p == n_pages // pps - 1)
    def _():
        pad = jnp.zeros((PAGE - t, BRANCH_W), F32)
        kn = jnp.concatenate([kn_ref[0], pad], axis=0).astype(BF16)
        vn = jnp.concatenate([vn_ref[0], pad], axis=0).astype(BF16)
        dist_n = (qpos - (p_len + tok)).astype(F32)
        mask_n = (dist_n >= 0) & (tok < t)
        m_n, l_n, pv_n = partial_softmax(_dot_nt(qrows[...], kn), dist_n, mask_n, vn)
        m_n = jnp.broadcast_to(m_n, wide)
        l_n = jnp.broadcast_to(l_n, wide)

        gates = [(gsum[2 * n] + gsum[2 * n + 1]) * (1.0 / MOBA_BLOCK) for n in range(n_blk)]
        sel = [jnp.zeros(wide, jnp.bool_) for _ in range(n_blk)]
        for _ in range(min(MOBA_TOPK, n_blk)):
            mx = functools.reduce(jnp.maximum, gates)
            found = jnp.zeros(wide, jnp.bool_)
            for n in range(n_blk):
                hit = (gates[n] == mx) & jnp.logical_not(found)
                found = found | hit
                sel[n] = sel[n] | hit
                gates[n] = jnp.where(hit, BELOW_NEG_INF, gates[n])

        m_tot = m_n
        for n in range(n_blk):
            for pg in (2 * n, 2 * n + 1):
                m_tot = jnp.maximum(m_tot, jnp.where(sel[n], m_pg[pg], NEG_INF))
        w_n = jnp.exp(m_n - m_tot)
        l_tot = l_n * w_n
        o = pv_n * w_n
        for n in range(n_blk):
            for pg in (2 * n, 2 * n + 1):
                w = jnp.where(sel[n], jnp.exp(m_pg[pg] - m_tot), 0.0)
                l_tot = l_tot + l_pg[pg] * w
                o = o + acc_pg[pg] * w
        o = o / l_tot
        for h in range(N_HEADS):
            o_ref[0, :, h * HEAD_DIM:(h + 1) * HEAD_DIM] = o[h * t:(h + 1) * t]


def _moba_decode(qn, kn, proj, cache_k, cache_v, page_table, layer):
    b, t, _ = qn.shape
    n_pages = page_table.shape[1]
    assert (n_pages * PAGE) % MOBA_BLOCK == 0 and t <= MOBA_BLOCK and MOBA_BLOCK == 2 * PAGE
    pps = PAGES_PER_STEP
    assert n_pages % pps == 0
    rows = N_HEADS * t
    slopes = np.repeat(_alibi_slopes(), t)
    sl_tab = jnp.asarray(np.broadcast_to(slopes[:, None], (rows, HEAD_DIM)), dtype=F32)
    stats = pltpu.VMEM((n_pages, rows, HEAD_DIM), F32)
    grid_spec = pltpu.PrefetchScalarGridSpec(
        num_scalar_prefetch=1,
        grid=(b, n_pages // pps),
        in_specs=[
            pl.BlockSpec((rows, HEAD_DIM), lambda i, p, pt: (0, 0)),
            pl.BlockSpec((1, t, BRANCH_W), lambda i, p, pt: (i, 0, 0)),
            pl.BlockSpec((1, t, BRANCH_W), lambda i, p, pt: (i, 0, 0)),
            pl.BlockSpec((1, t, BRANCH_W), lambda i, p, pt: (i, 0, C_MV)),
        ] + _page_specs(layer, n_pages, pps) + _page_specs(layer, n_pages, pps),
        out_specs=pl.BlockSpec((1, t, BRANCH_W), lambda i, p, pt: (i, 0, 0)),
        scratch_shapes=[pltpu.VMEM((rows, BRANCH_W), BF16), stats, stats, stats, stats],
    )
    return pl.pallas_call(
        functools.partial(_moba_dec_kernel, t=t, n_pages=n_pages, pps=pps),
        grid_spec=grid_spec,
        out_shape=jax.ShapeDtypeStruct((b, t, BRANCH_W), F32),
        compiler_params=_cparams("parallel", "arbitrary"),
        name="moba_decode",
    )(page_table.reshape(-1), sl_tab, qn, kn, proj, *([cache_k] * pps), *([cache_v] * pps))


CONV_PAD = 8


def _lru_gate_kernel(x_ref, buf_ref, cw_ref, cb_ref, wa_ref, ba_ref, wx_ref, bx_ref, lam_ref,
                     a_ref, u_ref, xe, *, tt):
    ti = pl.program_id(1)

    @pl.when(ti == 0)
    def _():
        xe[0:CONV_PAD, :] = buf_ref[0]

    @pl.when(ti > 0)
    def _():
        xe[0:CONV_PAD, :] = xe[tt:tt + CONV_PAD, :]

    xe[CONV_PAD:CONV_PAD + tt, :] = x_ref[0]
    cw = cw_ref[0]
    y = cb_ref[0]
    for i in range(CONV_W):
        off = CONV_PAD - (CONV_W - 1) + i
        y = y + xe[off:off + tt, :] * cw[i:i + 1]
    neg_lam = -lam_ref[0]
    softplus = jnp.maximum(neg_lam, 0.0) + jnp.log1p(jnp.exp(-jnp.abs(neg_lam)))
    for n in range(N_HEADS):
        sl = slice(n * HEAD_DIM, (n + 1) * HEAD_DIM)
        yn = y[:, sl]
        yb = yn.astype(BF16)
        r = _sigmoid(_dot(yb, wa_ref[0, n]) + ba_ref[0][:, sl])
        ig = _sigmoid(_dot(yb, wx_ref[0, n]) + bx_ref[0][:, sl])
        log_a = -LRU_C * r * softplus[:, sl]
        a = jnp.exp(log_a)
        a_ref[0, :, sl] = a
        u_ref[0, :, sl] = jnp.sqrt(jnp.tanh(-log_a) * (a * a + 1.0)) * (ig * yn)


def _lru_gates(proj, buf_pad, lw, layer, tt):
    b, t, _ = proj.shape
    vec = pl.BlockSpec((1, 1, BRANCH_W), lambda i, j: (layer, 0, 0))
    blk = pl.BlockSpec((1, N_HEADS, HEAD_DIM, HEAD_DIM), lambda i, j: (layer, 0, 0, 0))
    out = pl.BlockSpec((1, tt, BRANCH_W), lambda i, j: (i, j, 0))
    return pl.pallas_call(
        functools.partial(_lru_gate_kernel, tt=tt),
        grid=(b, t // tt),
        in_specs=[
            pl.BlockSpec((1, tt, BRANCH_W), lambda i, j: (i, j, C_LX)),
            pl.BlockSpec((1, CONV_PAD, BRANCH_W), lambda i, j: (i, 0, 0)),
            pl.BlockSpec((1, CONV_W, BRANCH_W), lambda i, j: (layer, 0, 0)),
            vec, blk, vec, blk, vec, vec,
        ],
        out_specs=[out, out],
        out_shape=[jax.ShapeDtypeStruct((b, t, BRANCH_W), F32)] * 2,
        scratch_shapes=[pltpu.VMEM((tt + CONV_PAD, BRANCH_W), F32)],
        compiler_params=_cparams("parallel", "arbitrary"),
        name="lru_gates",
    )(proj, buf_pad, lw["conv_w"], lw["conv_b"], lw["w_a"], lw["b_a"], lw["w_x"], lw["b_x"], lw["lam"])


def _gelu_tanh(x):
    return 0.5 * x * (1.0 + jnp.tanh(math.sqrt(2.0 / math.pi) * (x + 0.044715 * (x * x * x))))


def _lru_scan_kernel(a_ref, u_ref, g_ref, h0_ref, o_ref, ht_ref, h_scr, *, tt):
    ti = pl.program_id(1)

    @pl.when(ti == 0)
    def _():
        h_scr[...] = h0_ref[0]

    def body(i, h):
        h = a_ref[0, i] * h + u_ref[0, i]
        o_ref[0, i] = h
        return h

    h = lax.fori_loop(0, tt, body, h_scr[...], unroll=8)
    h_scr[...] = h
    ht_ref[0] = h
    o_ref[0] = o_ref[0] * _gelu_tanh(g_ref[0])


def _lru_scan(a, u, proj, h0, tt):
    b, t, _ = a.shape
    sub = BRANCH_W // HEAD_DIM
    a4 = a.reshape(b, t, sub, HEAD_DIM)
    u4 = u.reshape(b, t, sub, HEAD_DIM)
    g4 = proj[:, :, C_LG * BRANCH_W:(C_LG + 1) * BRANCH_W].reshape(b, t, sub, HEAD_DIM)
    blk = pl.BlockSpec((1, tt, sub, HEAD_DIM), lambda i, j: (i, j, 0, 0))
    out, h_t = pl.pallas_call(
        functools.partial(_lru_scan_kernel, tt=tt),
        grid=(b, t // tt),
        in_specs=[
            blk, blk, blk,
            pl.BlockSpec((1, sub, HEAD_DIM), lambda i, j: (i, 0, 0)),
        ],
        out_specs=[blk, pl.BlockSpec((1, sub, HEAD_DIM), lambda i, j: (i, 0, 0))],
        out_shape=[
            jax.ShapeDtypeStruct((b, t, sub, HEAD_DIM), F32),
            jax.ShapeDtypeStruct((b, sub, HEAD_DIM), F32),
        ],
        scratch_shapes=[pltpu.VMEM((sub, HEAD_DIM), F32)],
        compiler_params=_cparams("parallel", "arbitrary"),
        name="lru_scan",
    )(a4, u4, g4, h0.reshape(b, sub, HEAD_DIM))
    return out.reshape(b, t, BRANCH_W), h_t.reshape(b, BRANCH_W)


def _trunk_layer(x, mod, w, layer, lam_init, state, past, cfg):
    b, t, d = x.shape
    m = b * t
    shift1, scale1, gate1, shift2, scale2, gate2 = mod
    xm = x.reshape(cfg["mod_shape"])

    u = _normmod(xm, w["norm1_g"], layer, scale1, shift1, cfg["tt_norm"]).reshape(m, d)
    proj2 = _matmul(u, w["w_in"], layer, cfg["tm"], cfg["tn_in"])
    proj = proj2.reshape(b, t, proj2.shape[1])

    ret_s0, lru_h0, conv0 = state
    o_ret, ret_s = _retention(proj, ret_s0, w["ret_gn_g"], layer, cfg["ret_chunk"])

    tt = cfg["tt_head"]
    dq = _headnorm(proj, C_DQ, w["diff_qn_g"], layer, DIFF_DQK, tt)
    dk = _headnorm(proj, C_DK, w["diff_kn_g"], layer, DIFF_DQK, tt)
    mq = _headnorm(proj, C_MQ, w["moba_qn_g"], layer, HEAD_DIM, tt)
    mk = _headnorm(proj, C_MK, w["moba_kn_g"], layer, HEAD_DIM, tt)
    if past is None:
        o_dif = _diff_attn(dq, dk, proj, w["diff_lam"], w["diff_subln_g"], layer, lam_init, cfg["tq_diff"])
        o_moba = _moba_attn(mq, mk, proj)
    else:
        cdk, cdv, cmk, cmv, page_table = past
        o_dif = _diff_decode(dq, dk, proj, cdk, cdv, page_table, w["diff_lam"], w["diff_subln_g"], layer, lam_init)
        o_moba = _moba_decode(mq, mk, proj, cmk, cmv, page_table, layer)

    buf_pad = jnp.pad(conv0, ((0, 0), (CONV_PAD - (CONV_W - 1), 0), (0, 0)))
    a, uu = _lru_gates(proj, buf_pad, w["lru"], layer, cfg["tt_lru"])
    o_lru, lru_h = _lru_scan(a, uu, proj, lru_h0, cfg["tt_lru"])

    branch_outs = [o.reshape(m, BRANCH_W) for o in (o_ret, o_dif, o_moba, o_lru)]
    merged = _merge(branch_outs, proj2, w["w_br"], layer, cfg["tm_merge"], cfg["tn_merge"])
    x2 = _matmul_resid(merged, w["w_o"], layer, x.reshape(m, d), gate1, cfg["rows_per_gate"],
                       cfg["tm"], cfg["tn_out"])
    u2 = _normmod(x2.reshape(cfg["mod_shape"]), w["norm2_g"], layer, scale2, shift2, cfg["tt_norm"]).reshape(m, d)
    act = _matmul_swiglu(u2, w["w_ffn_in"], layer, cfg["tm"], cfg["tn_ffn"])
    x3 = _matmul_resid_ksplit(act, w["w_ffn_out"], layer, x2, gate2, cfg["rows_per_gate"],
                              cfg["tm"], cfg["tn_ffn_out"], cfg["tk_ffn"])

    lx = proj[:, :, C_LX * BRANCH_W:(C_LX + 1) * BRANCH_W]
    lru_conv = jnp.concatenate([conv0, lx], axis=1)[:, -(CONV_W - 1):]
    new_state = (
        dk.reshape(b, t, N_HEADS, HEAD_DIM),
        proj[:, :, C_DV * BRANCH_W:(C_DV + 1) * BRANCH_W].reshape(b, t, N_HEADS, HEAD_DIM),
        mk.reshape(b, t, N_HEADS, HEAD_DIM),
        proj[:, :, C_MV * BRANCH_W:(C_MV + 1) * BRANCH_W].reshape(b, t, N_HEADS, HEAD_DIM),
        ret_s, lru_h, lru_conv,
    )
    return x3.reshape(b, t, d), new_state


def _prompt_cfg(b, t, d, d_ff):
    return dict(
        mod_shape=(b, t, d), tt_norm=256, tm=1024, tn_in=512, tn_out=512, tn_ffn=256,
        tn_ffn_out=512, tk_ffn=d_ff // 2,
        ret_chunk=128 if t % 128 == 0 else t, tt_head=512, tq_diff=256, tt_lru=512,
        tm_merge=512, tn_merge=512, rows_per_gate=t,
    )


def _sample_cfg(b, t, d, d_ff):
    m = b * t
    return dict(
        mod_shape=(1, m, d), tt_norm=m, tm=m, tn_in=1024, tn_out=1024, tn_ffn=256,
        tn_ffn_out=1024, tk_ffn=d_ff // 2,
        ret_chunk=128 if t % 128 == 0 else t, tt_head=t, tt_lru=t,
        tm_merge=m, tn_merge=1024, rows_per_gate=m,
    )


def kernel(x_prompt, x_sample, c_prompt, c_sample, cache_dif_k, cache_dif_v, cache_moba_k, cache_moba_v, state_ret, state_lru_h, state_lru_conv, page_table, norm1_g, norm2_g, w_ada, b_ada, w_in, ret_gn_g, diff_qn_g, diff_kn_g, diff_lam, diff_subln_g, moba_qn_g, moba_kn_g, lru_conv_w, lru_conv_b, lru_w_a, lru_b_a, lru_w_x, lru_b_x, lru_lam, w_br, w_o, w_ffn_in, w_ffn_out):
    depth, d = norm1_g.shape
    bp, tp, _ = x_prompt.shape
    bs, ts, _ = x_sample.shape
    d_ff = w_ffn_out.shape[1]
    n_pool = cache_dif_k.shape[1]

    def vec(a):
        return a.reshape(depth, 1, a.shape[-1])

    w = dict(
        norm1_g=vec(norm1_g), norm2_g=vec(norm2_g),
        w_in=w_in, w_br=w_br, w_o=w_o, w_ffn_in=w_ffn_in, w_ffn_out=w_ffn_out.astype(BF16),
        ret_gn_g=vec(ret_gn_g),
        diff_qn_g=vec(jnp.tile(diff_qn_g, (1, 2))), diff_kn_g=vec(jnp.tile(diff_kn_g, (1, 2))),
        diff_lam=diff_lam, diff_subln_g=vec(diff_subln_g),
        moba_qn_g=vec(moba_qn_g), moba_kn_g=vec(moba_kn_g),
        lru=dict(conv_w=lru_conv_w, conv_b=vec(lru_conv_b), w_a=lru_w_a.astype(BF16), b_a=vec(lru_b_a),
                 w_x=lru_w_x.astype(BF16), b_x=vec(lru_b_x), lam=vec(lru_lam)),
    )

    n_c = bp + bs
    rows = -(-n_c // 8) * 8
    c_all = jnp.pad(jnp.concatenate([c_prompt, c_sample], axis=0), ((0, rows - n_c), (0, 0)))
    mod_all = _ada_mod(c_all, w_ada, b_ada).reshape(depth, rows, N_ADA, d)

    flat = lambda a: a.reshape(a.shape[0], a.shape[1], PAGE, BRANCH_W)
    cdk, cdv, cmk, cmv = flat(cache_dif_k), flat(cache_dif_v), flat(cache_moba_k), flat(cache_moba_v)

    cfg_p = _prompt_cfg(bp, tp, d, d_ff)
    cfg_s = _sample_cfg(bs, ts, d, d_ff)
    zero_state = (jnp.zeros((bp, N_HEADS, HEAD_DIM, HEAD_DIM), F32), jnp.zeros((bp, BRANCH_W), F32),
                  jnp.zeros((bp, CONV_W - 1, BRANCH_W), F32))

    y_p, y_s = x_prompt, x_sample
    new_p = [[] for _ in range(7)]
    new_s = [[] for _ in range(7)]
    for l in range(depth):
        lam_init = 0.8 - 0.6 * math.exp(-0.3 * l)
        mod_p = [mod_all[l, :bp, i][:, None, :] for i in range(N_ADA)]
        mod_s = [jnp.repeat(mod_all[l, bp:n_c, i], ts, axis=0)[None] for i in range(N_ADA)]
        y_p, st_p = _trunk_layer(y_p, mod_p, w, l, lam_init, zero_state, None, cfg_p)
        y_s, st_s = _trunk_layer(y_s, mod_s, w, l, lam_init,
                                 (state_ret[l], state_lru_h[l], state_lru_conv[l]),
                                 (cdk, cdv, cmk, cmv, page_table), cfg_s)
        for acc, a in zip(new_p, st_p):
            acc.append(a)
        for acc, a in zip(new_s, st_s):
            acc.append(a)
    outs_p = [jnp.stack(a) for a in new_p]
    outs_s = [jnp.stack(a) for a in new_s]
    return (y_p, y_s, *outs_p, *outs_s)
```

```python
import functools
import math

import numpy as np
import jax
import jax.numpy as jnp
from jax import lax
from jax.experimental import pallas as pl
from jax.experimental.pallas import tpu as pltpu

F32 = jnp.float32
BF16 = jnp.bfloat16

HEAD_DIM = 128
N_HEADS = 8
BRANCH_W = N_HEADS * HEAD_DIM
N_BRANCH = 4
DIFF_DQK = HEAD_DIM // 2
MOBA_BLOCK = 256
MOBA_TOPK = 3
CONV_W = 4
LRU_C = 8.0
N_ADA = 6
EPS = 1e-6
NEG_INF = -1e30
BELOW_NEG_INF = -3e38
PAGE = 128
VMEM_LIMIT = 56 * 1024 * 1024

C_RQ, C_RK, C_RV, C_RG, C_DQ, C_DK, C_DV, C_MQ, C_MK, C_MV, C_LX, C_LG, C_GATES = range(13)


def _cparams(*sem):
    return pltpu.CompilerParams(dimension_semantics=sem, vmem_limit_bytes=VMEM_LIMIT)


def _sigmoid(x):
    return 1.0 / (1.0 + jnp.exp(-x))


def _dot(a, b):
    return jnp.dot(a, b, preferred_element_type=F32)


def _dot_nt(a, b):
    return lax.dot_general(a, b, (((1,), (1,)), ((), ())), preferred_element_type=F32)


def _dot_tn(a, b):
    return lax.dot_general(a, b, (((0,), (0,)), ((), ())), preferred_element_type=F32)


def _ada_kernel(c_ref, w_ref, b_ref, o_ref):
    c = c_ref[...]
    a = (c * _sigmoid(c)).astype(BF16)
    o_ref[0] = _dot(a, w_ref[0].astype(BF16)) + b_ref[0]


def _ada_mod(c_all, w_ada, b_ada):
    depth, d, n = w_ada.shape
    r = c_all.shape[0]
    tn = 1024
    return pl.pallas_call(
        _ada_kernel,
        grid=(depth, n // tn),
        in_specs=[
            pl.BlockSpec((r, d), lambda l, j: (0, 0)),
            pl.BlockSpec((1, d, tn), lambda l, j: (l, 0, j)),
            pl.BlockSpec((1, 1, tn), lambda l, j: (l, 0, j)),
        ],
        out_specs=pl.BlockSpec((1, r, tn), lambda l, j: (l, 0, j)),
        out_shape=jax.ShapeDtypeStruct((depth, r, n), F32),
        compiler_params=_cparams("parallel", "parallel"),
        name="ada_mod",
    )(c_all, w_ada, b_ada.reshape(depth, 1, n))


def _normmod_kernel(x_ref, g_ref, sc_ref, sh_ref, o_ref):
    x = x_ref[0]
    y = x * lax.rsqrt(jnp.mean(x * x, axis=-1, keepdims=True) + EPS)
    y = y * g_ref[0]
    o_ref[0] = (y * (1.0 + sc_ref[0]) + sh_ref[0]).astype(o_ref.dtype)


def _normmod(x, g_all, layer, scale, shift, tt):
    b, t, d = x.shape
    r = scale.shape[1]
    rb = 1 if r == 1 else tt
    mod_map = (lambda i, j: (i, 0, 0)) if r == 1 else (lambda i, j: (i, j, 0))
    return pl.pallas_call(
        _normmod_kernel,
        grid=(b, t // tt),
        in_specs=[
            pl.BlockSpec((1, tt, d), lambda i, j: (i, j, 0)),
            pl.BlockSpec((1, 1, d), lambda i, j: (layer, 0, 0)),
            pl.BlockSpec((1, rb, d), mod_map),
            pl.BlockSpec((1, rb, d), mod_map),
        ],
        out_specs=pl.BlockSpec((1, tt, d), lambda i, j: (i, j, 0)),
        out_shape=jax.ShapeDtypeStruct((b, t, d), BF16),
        compiler_params=_cparams("parallel", "parallel"),
        name="norm_mod",
    )(x, g_all, scale, shift)


def _stage_weight(b_ref, wb_s):
    @pl.when(pl.program_id(1) == 0)
    def _():
        wb_s[...] = b_ref[0].astype(BF16)


def _mm_kernel(a_ref, b_ref, o_ref, wb_s):
    _stage_weight(b_ref, wb_s)
    o_ref[...] = _dot(a_ref[...], wb_s[...]).astype(o_ref.dtype)


def _matmul(a, w_all, layer, tm, tn):
    m, kk = a.shape
    n = w_all.shape[2]
    return pl.pallas_call(
        _mm_kernel,
        grid=(n // tn, m // tm),
        in_specs=[
            pl.BlockSpec((tm, kk), lambda j, i: (i, 0)),
            pl.BlockSpec((1, kk, tn), lambda j, i: (layer, 0, j)),
        ],
        out_specs=pl.BlockSpec((tm, tn), lambda j, i: (i, j)),
        out_shape=jax.ShapeDtypeStruct((m, n), F32),
        scratch_shapes=[pltpu.VMEM((kk, tn), BF16)],
        compiler_params=_cparams("parallel", "arbitrary"),
        name="matmul",
    )(a, w_all)


def _mm_resid_kernel(a_ref, b_ref, x_ref, g_ref, o_ref, wb_s):
    _stage_weight(b_ref, wb_s)
    o_ref[...] = x_ref[...] + g_ref[0] * _dot(a_ref[...], wb_s[...])


def _matmul_resid(a, w_all, layer, x, gate, rows_per_gate, tm, tn):
    m, kk = a.shape
    n = w_all.shape[2]
    r = gate.shape[1]
    tiles_per_gate = max(rows_per_gate // tm, 1)
    return pl.pallas_call(
        _mm_resid_kernel,
        grid=(n // tn, m // tm),
        in_specs=[
            pl.BlockSpec((tm, kk), lambda j, i: (i, 0)),
            pl.BlockSpec((1, kk, tn), lambda j, i: (layer, 0, j)),
            pl.BlockSpec((tm, tn), lambda j, i: (i, j)),
            pl.BlockSpec((1, r, tn), lambda j, i: (i // tiles_per_gate, 0, j)),
        ],
        out_specs=pl.BlockSpec((tm, tn), lambda j, i: (i, j)),
        out_shape=jax.ShapeDtypeStruct((m, n), F32),
        scratch_shapes=[pltpu.VMEM((kk, tn), BF16)],
        compiler_params=_cparams("parallel", "arbitrary"),
        name="matmul_resid",
    )(a, w_all, x, gate)


def _mm_resid_ksplit_kernel(a_ref, b_ref, x_ref, g_ref, o_ref, acc_ref, *, nk):
    k = pl.program_id(2)
    d = _dot(a_ref[...], b_ref[0])

    def finish(acc):
        o_ref[...] = x_ref[...] + g_ref[0] * acc

    if nk == 1:
        finish(d)
        return

    @pl.when(k == 0)
    def _():
        acc_ref[...] = d

    @pl.when(k > 0)
    def _():
        acc_ref[...] += d

    @pl.when(k == nk - 1)
    def _():
        finish(acc_ref[...])


def _matmul_resid_ksplit(a, w_all, layer, x, gate, rows_per_gate, tm, tn, tk):
    m, kk = a.shape
    n = w_all.shape[2]
    nk = kk // tk
    r = gate.shape[1]
    tiles_per_gate = max(rows_per_gate // tm, 1)
    return pl.pallas_call(
        functools.partial(_mm_resid_ksplit_kernel, nk=nk),
        grid=(m // tm, n // tn, nk),
        in_specs=[
            pl.BlockSpec((tm, tk), lambda i, j, k: (i, k)),
            pl.BlockSpec((1, tk, tn), lambda i, j, k: (layer, k, j)),
            pl.BlockSpec((tm, tn), lambda i, j, k: (i, j)),
            pl.BlockSpec((1, r, tn), lambda i, j, k: (i // tiles_per_gate, 0, j)),
        ],
        out_specs=pl.BlockSpec((tm, tn), lambda i, j, k: (i, j)),
        out_shape=jax.ShapeDtypeStruct((m, n), F32),
        scratch_shapes=[pltpu.VMEM((tm, tn) if nk > 1 else (8, 128), F32)],
        compiler_params=_cparams("parallel", "parallel", "arbitrary"),
        name="matmul_resid",
    )(a, w_all, x, gate)


def _mm_swiglu_kernel(a_ref, bg_ref, bu_ref, o_ref, wg_s, wu_s):
    _stage_weight(bg_ref, wg_s)
    _stage_weight(bu_ref, wu_s)
    a = a_ref[...]
    hg = _dot(a, wg_s[...])
    hu = _dot(a, wu_s[...])
    o_ref[...] = (hg * _sigmoid(hg) * hu).astype(o_ref.dtype)


def _matmul_swiglu(a, w_all, layer, tm, tn):
    m, kk = a.shape
    f = w_all.shape[2] // 2
    nj = f // tn
    panel = pltpu.VMEM((kk, tn), BF16)
    return pl.pallas_call(
        _mm_swiglu_kernel,
        grid=(nj, m // tm),
        in_specs=[
            pl.BlockSpec((tm, kk), lambda j, i: (i, 0)),
            pl.BlockSpec((1, kk, tn), lambda j, i: (layer, 0, j)),
            pl.BlockSpec((1, kk, tn), lambda j, i: (layer, 0, j + nj)),
        ],
        out_specs=pl.BlockSpec((tm, tn), lambda j, i: (i, j)),
        out_shape=jax.ShapeDtypeStruct((m, f), BF16),
        scratch_shapes=[panel, panel],
        compiler_params=_cparams("parallel", "arbitrary"),
        name="matmul_swiglu",
    )(a, w_all, w_all)


def _merge_kernel(o0, o1, o2, o3, g0, g1, g2, g3, w_ref, out_ref, wb_s):
    _stage_weight(w_ref, wb_s)
    acc = None
    for i, (o, g) in enumerate(((o0, g0), (o1, g1), (o2, g2), (o3, g3))):
        t = _sigmoid(g[...]) * _dot(o[...].astype(BF16), wb_s[i])
        acc = t if acc is None else acc + t
    out_ref[...] = acc.astype(out_ref.dtype)


def _merge(branch_outs, proj, w_br_all, layer, tm, tn):
    m = proj.shape[0]
    d = w_br_all.shape[3]
    gate_blk0 = C_GATES * BRANCH_W // tn
    per_branch = d // tn
    o_specs = [pl.BlockSpec((tm, BRANCH_W), lambda j, i: (i, 0)) for _ in range(N_BRANCH)]
    g_specs = [
        pl.BlockSpec((tm, tn), functools.partial(lambda j, i, br: (i, gate_blk0 + br * per_branch + j), br=br))
        for br in range(N_BRANCH)
    ]
    return pl.pallas_call(
        _merge_kernel,
        grid=(d // tn, m // tm),
        in_specs=o_specs + g_specs + [pl.BlockSpec((1, N_BRANCH, BRANCH_W, tn), lambda j, i: (layer, 0, 0, j))],
        out_specs=pl.BlockSpec((tm, tn), lambda j, i: (i, j)),
        out_shape=jax.ShapeDtypeStruct((m, d), BF16),
        scratch_shapes=[pltpu.VMEM((N_BRANCH, BRANCH_W, tn), BF16)],
        compiler_params=_cparams("parallel", "arbitrary"),
        name="merge",
    )(*branch_outs, proj, proj, proj, proj, w_br_all)


def _head_rms(xh, g, group):
    x2 = xh * xh
    if group == HEAD_DIM:
        ms = jnp.mean(x2, axis=-1, keepdims=True)
    else:
        lo = lax.broadcasted_iota(jnp.int32, xh.shape, 1) < group
        s_lo = jnp.sum(jnp.where(lo, x2, 0.0), axis=-1, keepdims=True)
        s_hi = jnp.sum(jnp.where(lo, 0.0, x2), axis=-1, keepdims=True)
        ms = jnp.where(lo, s_lo, s_hi) * (1.0 / group)
    return xh * lax.rsqrt(ms + EPS) * g


def _headnorm_kernel(x_ref, g_ref, o_ref, *, group):
    g = g_ref[0]
    for h in range(N_HEADS):
        sl = slice(h * HEAD_DIM, (h + 1) * HEAD_DIM)
        o_ref[0, :, sl] = _head_rms(x_ref[0, :, sl], g, group)


def _kv_prep_kernel(k_ref, v_ref, g_ref, kn_ref, kn4_ref, v4_ref, *, group):
    g = g_ref[0]
    for h in range(N_HEADS):
        sl = slice(h * HEAD_DIM, (h + 1) * HEAD_DIM)
        kn = _head_rms(k_ref[0, :, sl], g, group)
        kn_ref[0, :, sl] = kn
        kn4_ref[0, :, h, :] = kn
        v4_ref[0, :, h, :] = v_ref[0, :, sl]


def _kv_prep(proj, kcol, vcol, g_all, layer, group, tt):
    b, t, _ = proj.shape
    flat = pl.BlockSpec((1, tt, BRANCH_W), lambda i, j: (i, j, 0))
    per_head = pl.BlockSpec((1, tt, N_HEADS, HEAD_DIM), lambda i, j: (i, j, 0, 0))
    cache_shape = jax.ShapeDtypeStruct((b, t, N_HEADS, HEAD_DIM), F32)
    return pl.pallas_call(
        functools.partial(_kv_prep_kernel, group=group),
        grid=(b, t // tt),
        in_specs=[
            pl.BlockSpec((1, tt, BRANCH_W), lambda i, j: (i, j, kcol)),
            pl.BlockSpec((1, tt, BRANCH_W), lambda i, j: (i, j, vcol)),
            pl.BlockSpec((1, 1, HEAD_DIM), lambda i, j: (layer, 0, 0)),
        ],
        out_specs=[flat, per_head, per_head],
        out_shape=[jax.ShapeDtypeStruct((b, t, BRANCH_W), F32), cache_shape, cache_shape],
        compiler_params=_cparams("parallel", "parallel"),
        name="kv_prep",
    )(proj, proj, g_all)


def _headnorm(proj, col, g_all, layer, group, tt):
    b, t, _ = proj.shape
    return pl.pallas_call(
        functools.partial(_headnorm_kernel, group=group),
        grid=(b, t // tt),
        in_specs=[
            pl.BlockSpec((1, tt, BRANCH_W), lambda i, j: (i, j, col)),
            pl.BlockSpec((1, 1, HEAD_DIM), lambda i, j: (layer, 0, 0)),
        ],
        out_specs=pl.BlockSpec((1, tt, BRANCH_W), lambda i, j: (i, j, 0)),
        out_shape=jax.ShapeDtypeStruct((b, t, BRANCH_W), F32),
        compiler_params=_cparams("parallel", "parallel"),
        name="head_norm",
    )(proj, g_all)


def _ret_kernel(q_ref, k_ref, v_ref, g_ref, s0_ref, gn_ref, o_ref, s_out_ref, s_scr, *, c, nc):
    ci = pl.program_id(1)

    @pl.when(ci == 0)
    def _():
        s_scr[...] = s0_ref[0]

    ii = lax.broadcasted_iota(jnp.int32, (c, c), 0)
    jj = lax.broadcasted_iota(jnp.int32, (c, c), 1)
    dist = (ii - jj).astype(F32)
    idx = lax.broadcasted_iota(jnp.int32, (c, 1), 0).astype(F32)
    log_decay = np.log(1.0 - 2.0 ** (-5.0 - np.arange(N_HEADS)))
    for h in range(N_HEADS):
        lg = float(np.float32(log_decay[h]))
        sl = slice(h * HEAD_DIM, (h + 1) * HEAD_DIM)
        decay = jnp.where(dist >= 0, jnp.exp(lg * jnp.maximum(dist, 0.0)), 0.0)
        q_decay = jnp.exp(lg * (idx + 1.0))
        k_decay = jnp.exp(lg * (c - 1.0 - idx))
        s_decay = math.exp(lg * c)

        q = q_ref[0, :, sl]
        k = k_ref[0, :, sl] * HEAD_DIM ** -0.5
        qb = q.astype(BF16)
        vb = v_ref[0, :, sl].astype(BF16)
        s = s_scr[h]
        inner = _dot_nt(qb, k.astype(BF16)) * decay
        o = _dot(inner.astype(BF16), vb) + _dot(qb, s.astype(BF16)) * q_decay
        s_scr[h] = s * s_decay + _dot_tn((k * k_decay).astype(BF16), vb)

        mu = jnp.mean(o, axis=-1, keepdims=True)
        var = jnp.mean(jnp.square(o - mu), axis=-1, keepdims=True)
        o = (o - mu) * lax.rsqrt(var + EPS) * gn_ref[0]
        g = g_ref[0, :, sl]
        o_ref[0, :, sl] = (o * (g * _sigmoid(g))).astype(o_ref.dtype)

    @pl.when(ci == nc - 1)
    def _():
        s_out_ref[0] = s_scr[...]


def _retention(proj, s0, gn_all, layer, c):
    b, t, _ = proj.shape
    nc = t // c

    def col_spec(col):
        return pl.BlockSpec((1, c, BRANCH_W), lambda i, j: (i, j, col))

    state_spec = pl.BlockSpec((1, N_HEADS, HEAD_DIM, HEAD_DIM), lambda i, j: (i, 0, 0, 0))
    return pl.pallas_call(
        functools.partial(_ret_kernel, c=c, nc=nc),
        grid=(b, nc),
        in_specs=[
            col_spec(C_RQ), col_spec(C_RK), col_spec(C_RV), col_spec(C_RG),
            state_spec,
            pl.BlockSpec((1, 1, HEAD_DIM), lambda i, j: (layer, 0, 0)),
        ],
        out_specs=[pl.BlockSpec((1, c, BRANCH_W), lambda i, j: (i, j, 0)), state_spec],
        out_shape=[
            jax.ShapeDtypeStruct((b, t, BRANCH_W), BF16 if c % 16 == 0 else F32),
            jax.ShapeDtypeStruct((b, N_HEADS, HEAD_DIM, HEAD_DIM), F32),
        ],
        scratch_shapes=[pltpu.VMEM((N_HEADS, HEAD_DIM, HEAD_DIM), F32)],
        compiler_params=_cparams("parallel", "arbitrary"),
        name="retention",
    )(proj, proj, proj, proj, s0, gn_all)


def _alibi_slopes():
    return 2.0 ** (-8.0 * np.arange(1, N_HEADS + 1) / N_HEADS)


def _diff_lambda(lam_ref, lam_init):
    lp = lam_ref[0]
    s01 = jnp.sum(lp[0:1] * lp[1:2], axis=-1, keepdims=True)
    s23 = jnp.sum(lp[2:3] * lp[3:4], axis=-1, keepdims=True)
    return jnp.exp(s01) - jnp.exp(s23) + lam_init


def _softmax_update(s, v_bf16, m_ref, l_ref, acc_ref):
    m_old = m_ref[...]
    m_new = jnp.maximum(m_old, jnp.max(s, axis=-1, keepdims=True))
    alpha = jnp.exp(m_old - m_new)
    p = jnp.exp(s - m_new)
    l_ref[...] = alpha * l_ref[...] + jnp.sum(p, axis=-1, keepdims=True)
    acc_ref[...] = alpha * acc_ref[...] + _dot(p.astype(BF16), v_bf16)
    m_ref[...] = m_new


def _subln(o, g, lam_init):
    return o * lax.rsqrt(jnp.mean(o * o, axis=-1, keepdims=True) + EPS) * g * (1.0 - lam_init)


def _stage_kv(k_ref, v_ref, kb_s, vt_s, tk):
    kb_s[...] = k_ref[0].astype(BF16)
    for n in range(kb_s.shape[0] // tk):
        vt_s[n] = v_ref[0, n * tk:(n + 1) * tk, :].T.astype(BF16)


def _softmax_update_t(s, shift, vt_bf16, m_ref, l_ref, acc_ref):
    m_old = m_ref[...]
    m_new = jnp.maximum(m_old, jnp.max(s, axis=0, keepdims=True) - shift)
    alpha = jnp.exp(m_old - m_new)
    p = jnp.exp(s - (m_new + shift))
    l_ref[...] = alpha * l_ref[...] + jnp.sum(p, axis=0, keepdims=True)
    acc_ref[...] = alpha * acc_ref[...] + _dot(vt_bf16, p.astype(BF16))
    m_ref[...] = m_new


def _tile_shift(slope, tiles_apart, tq, width):
    return slope * (tiles_apart * tq + jnp.zeros((1, width), jnp.int32)).astype(F32)


def _diff_attn_kernel(sl_ref, lam_ref, q_ref, k_ref, v_ref, sg_ref, o_ref,
                      kb_s, vt_s, bias_s, m_s, l_s, acc, *, tq, lam_init):
    qi = pl.program_id(2)

    slope = sl_ref[0][:, :1]
    rel = lax.broadcasted_iota(jnp.int32, (tq, tq), 1) - lax.broadcasted_iota(jnp.int32, (tq, tq), 0)
    rel2 = jnp.concatenate([rel, rel], axis=1)

    @pl.when(qi == 0)
    def _():
        _stage_kv(k_ref, v_ref, kb_s, vt_s, tq)
        bias_s[0:tq, :] = slope * rel2.astype(F32)
        bias_s[tq:2 * tq, :] = slope * (rel2 - tq).astype(F32)

    qt = (q_ref[0] * DIFF_DQK ** -0.5).T
    chan = lax.broadcasted_iota(jnp.int32, (HEAD_DIM, tq), 0)
    qcat = jnp.concatenate([jnp.where(chan < DIFF_DQK, qt, 0.0), jnp.where(chan < DIFF_DQK, 0.0, qt)],
                           axis=1).astype(BF16)
    m_s[...] = jnp.full(m_s.shape, NEG_INF, F32)
    l_s[...] = jnp.zeros(l_s.shape, F32)
    acc[...] = jnp.zeros(acc.shape, F32)

    def span(j, nt, masked):
        start = pl.multiple_of(j * tq, tq)
        s = _dot(kb_s[pl.ds(start, nt * tq), :], qcat) - bias_s[0:nt * tq, :]
        if masked:
            s = jnp.where(rel2 >= 0, s, NEG_INF)
        vt = vt_s[j] if nt == 1 else jnp.concatenate([vt_s[j], vt_s[j + 1]], axis=1)
        _softmax_update_t(s, _tile_shift(slope, qi - j, tq, 2 * tq), vt, m_s, l_s, acc)

    span(qi, 1, True)

    def body(jj, carry):
        span(2 * jj, 2, False)
        return carry

    lax.fori_loop(0, qi // 2, body, 0)

    @pl.when(qi % 2 == 1)
    def _():
        span(qi - 1, 1, False)

    lam = _diff_lambda(lam_ref, lam_init)
    o = acc[...] / l_s[...]
    o = (o[:, 0:tq] - lam * o[:, tq:2 * tq]).T
    o_ref[0] = _subln(o, sg_ref[0], lam_init).astype(o_ref.dtype)


def _diff_attn(qn, kn, proj, lam_all, sg_all, layer, lam_init, tq):
    b, t, _ = qn.shape
    sl_tab = jnp.asarray(np.broadcast_to(_alibi_slopes()[:, None, None], (N_HEADS, 1, HEAD_DIM)), dtype=F32)
    stat = pltpu.VMEM((1, 2 * tq), F32)
    return pl.pallas_call(
        functools.partial(_diff_attn_kernel, tq=tq, lam_init=lam_init),
        grid=(b, N_HEADS, t // tq),
        in_specs=[
            pl.BlockSpec((1, 1, HEAD_DIM), lambda i, h, j: (h, 0, 0)),
            pl.BlockSpec((1, 4, DIFF_DQK), lambda i, h, j: (layer, 0, 0)),
            pl.BlockSpec((1, tq, HEAD_DIM), lambda i, h, j: (i, j, h)),
            pl.BlockSpec((1, t, HEAD_DIM), lambda i, h, j: (i, 0, h)),
            pl.BlockSpec((1, t, HEAD_DIM), lambda i, h, j: (i, 0, C_DV * N_HEADS + h)),
            pl.BlockSpec((1, 1, HEAD_DIM), lambda i, h, j: (layer, 0, 0)),
        ],
        out_specs=pl.BlockSpec((1, tq, HEAD_DIM), lambda i, h, j: (i, j, h)),
        out_shape=jax.ShapeDtypeStruct((b, t, BRANCH_W), BF16),
        scratch_shapes=[pltpu.VMEM((t, HEAD_DIM), BF16), pltpu.VMEM((t // tq, HEAD_DIM, tq), BF16),
                        pltpu.VMEM((2 * tq, 2 * tq), F32), stat, stat, pltpu.VMEM((HEAD_DIM, 2 * tq), F32)],
        compiler_params=_cparams("parallel", "parallel", "arbitrary"),
        name="diff_attn",
    )(sl_tab, lam_all, qn, kn, proj, sg_all)


def _split_bf16(x):
    hi = x.astype(BF16)
    lo = (x - hi.astype(F32)).astype(BF16)
    return hi, lo


def _moba_attn_kernel(sl_ref, q_ref, k_ref, v_ref, o_ref, kb_s, vt_s, bias_s, kmean, allowed_s, m_s, l_s, acc, *, nb):
    tq = MOBA_BLOCK
    qi = pl.program_id(2)
    slope = sl_ref[0][:, :1]
    nbp = kmean.shape[0]
    rel = lax.broadcasted_iota(jnp.int32, (tq, tq), 1) - lax.broadcasted_iota(jnp.int32, (tq, tq), 0)

    @pl.when(qi == 0)
    def _():
        _stage_kv(k_ref, v_ref, kb_s, vt_s, tq)
        bias_s[0:tq, :] = slope * rel.astype(F32)
        bias_s[tq:2 * tq, :] = slope * (rel - tq).astype(F32)
        kmean[...] = jnp.zeros(kmean.shape, F32)
        for n in range(nb):
            kmean[n:n + 1, :] = jnp.mean(k_ref[0, n * tq:(n + 1) * tq, :], axis=0, keepdims=True)

    qt = q_ref[0].T
    qb = qt.astype(BF16)
    qh, ql = _split_bf16(qt)
    kh, kl = _split_bf16(kmean[...])
    gate = _dot(kh, qh) + (_dot(kl, qh) + _dot(kh, ql))
    blk = lax.broadcasted_iota(jnp.int32, (nbp, tq), 0)
    g = jnp.where(blk < qi, gate, NEG_INF)
    allowed = jnp.zeros((nbp, tq), F32)
    for _ in range(MOBA_TOPK):
        mx = jnp.max(g, axis=0, keepdims=True)
        first = jnp.min(jnp.where(g == mx, blk, nbp), axis=0, keepdims=True)
        hit = blk == first
        allowed = jnp.where(hit & (mx > 0.5 * NEG_INF), 1.0, allowed)
        g = jnp.where(hit, BELOW_NEG_INF, g)
    allowed_s[...] = allowed

    m_s[...] = jnp.full(m_s.shape, NEG_INF, F32)
    l_s[...] = jnp.zeros(l_s.shape, F32)
    acc[...] = jnp.zeros(acc.shape, F32)
    scale = HEAD_DIM ** -0.5

    def span(j, nt, own):
        start = pl.multiple_of(j * tq, tq)
        s = _dot(kb_s[pl.ds(start, nt * tq), :], qb) * scale - bias_s[0:nt * tq, :]
        if own:
            s = jnp.where(rel >= 0, s, NEG_INF)
        else:
            picked = [jnp.broadcast_to(allowed_s[pl.ds(j + i, 1), :], (tq, tq)) for i in range(nt)]
            s = jnp.where(jnp.concatenate(picked, axis=0) > 0.0, s, NEG_INF)
        vt = vt_s[j] if nt == 1 else jnp.concatenate([vt_s[j], vt_s[j + 1]], axis=1)
        _softmax_update_t(s, _tile_shift(slope, qi - j, tq, tq), vt, m_s, l_s, acc)

    span(qi, 1, True)

    def body(jj, carry):
        span(2 * jj, 2, False)
        return carry

    lax.fori_loop(0, qi // 2, body, 0)

    @pl.when(qi % 2 == 1)
    def _():
        span(qi - 1, 1, False)

    o_ref[0] = (acc[...] / l_s[...]).T.astype(o_ref.dtype)


def _moba_attn(qn, kn, proj):
    b, t, _ = qn.shape
    assert t % MOBA_BLOCK == 0
    nb = t // MOBA_BLOCK
    nbp = -(-nb // 8) * 8
    sl_tab = jnp.asarray(np.broadcast_to(_alibi_slopes()[:, None, None], (N_HEADS, 1, HEAD_DIM)), dtype=F32)
    return pl.pallas_call(
        functools.partial(_moba_attn_kernel, nb=nb),
        grid=(b, N_HEADS, nb),
        in_specs=[
            pl.BlockSpec((1, 1, HEAD_DIM), lambda i, h, j: (h, 0, 0)),
            pl.BlockSpec((1, MOBA_BLOCK, HEAD_DIM), lambda i, h, j: (i, j, h)),
            pl.BlockSpec((1, t, HEAD_DIM), lambda i, h, j: (i, 0, h)),
            pl.BlockSpec((1, t, HEAD_DIM), lambda i, h, j: (i, 0, C_MV * N_HEADS + h)),
        ],
        out_specs=pl.BlockSpec((1, MOBA_BLOCK, HEAD_DIM), lambda i, h, j: (i, j, h)),
        out_shape=jax.ShapeDtypeStruct((b, t, BRANCH_W), BF16),
        scratch_shapes=[
            pltpu.VMEM((t, HEAD_DIM), BF16),
            pltpu.VMEM((nb, HEAD_DIM, MOBA_BLOCK), BF16),
            pltpu.VMEM((2 * MOBA_BLOCK, MOBA_BLOCK), F32),
            pltpu.VMEM((nbp, HEAD_DIM), F32),
            pltpu.VMEM((nbp, MOBA_BLOCK), F32),
            pltpu.VMEM((1, MOBA_BLOCK), F32),
            pltpu.VMEM((1, MOBA_BLOCK), F32),
            pltpu.VMEM((HEAD_DIM, MOBA_BLOCK), F32),
        ],
        compiler_params=_cparams("parallel", "parallel", "arbitrary"),
        name="moba_attn",
    )(sl_tab, qn, kn, proj)


def _query_rows(q, maps):
    t = q.shape[0]
    rows = N_HEADS * maps * t
    width = HEAD_DIM // maps
    tiled = jnp.concatenate([q] * (N_HEADS * maps), axis=0)
    r = lax.broadcasted_iota(jnp.int32, (rows, BRANCH_W), 0)
    ch = lax.broadcasted_iota(jnp.int32, (rows, BRANCH_W), 1)
    return jnp.where(r // t == ch // width, tiled, 0.0)


PAGES_PER_STEP = 4


def _page_specs(layer, n_pages, pps):
    def spec(k):
        return pl.BlockSpec((1, 1, PAGE, BRANCH_W),
                            lambda i, p, pt: (layer, pt[i * n_pages + p * pps + k], 0, 0))
    return [spec(k) for k in range(pps)]


def _diff_dec_kernel(pt_ref, sl_ref, lam_ref, q_ref, kn_ref, vn_ref, sg_ref, *rest, t, n_pages, pps, lam_init):
    kc_refs, vc_refs = rest[:pps], rest[pps:2 * pps]
    o_ref, qrows, m_s, l_s, acc = rest[2 * pps:]
    p = pl.program_id(1)
    rows = N_HEADS * 2 * t
    p_len = n_pages * PAGE
    span = pps * PAGE

    @pl.when(p == 0)
    def _():
        qrows[...] = _query_rows(q_ref[0] * DIFF_DQK ** -0.5, 2).astype(BF16)
        m_s[...] = jnp.full(m_s.shape, NEG_INF, F32)
        l_s[...] = jnp.zeros(l_s.shape, F32)
        acc[...] = jnp.zeros(acc.shape, F32)

    slope = sl_ref[:, :1]
    qpos_w = p_len + lax.broadcasted_iota(jnp.int32, (rows, span), 0) % t
    tok_w = lax.broadcasted_iota(jnp.int32, (rows, span), 1)
    dist = (qpos_w - (p * span + tok_w)).astype(F32)
    kb = jnp.concatenate([r[0, 0] for r in kc_refs], axis=0).astype(BF16)
    vb = jnp.concatenate([r[0, 0] for r in vc_refs], axis=0).astype(BF16)
    s = _dot_nt(qrows[...], kb) - slope * dist
    _softmax_update(s, vb, m_s, l_s, acc)

    @pl.when(p == n_pages // pps - 1)
    def _():
        qpos = p_len + lax.broadcasted_iota(jnp.int32, (rows, PAGE), 0) % t
        tok = lax.broadcasted_iota(jnp.int32, (rows, PAGE), 1)
        pad = jnp.zeros((PAGE - t, BRANCH_W), F32)
        kn = jnp.concatenate([kn_ref[0], pad], axis=0).astype(BF16)
        vn = jnp.concatenate([vn_ref[0], pad], axis=0).astype(BF16)
        dist_n = (qpos - (p_len + tok)).astype(F32)
        s_n = _dot_nt(qrows[...], kn) - slope * dist_n
        s_n = jnp.where((dist_n >= 0) & (tok < t), s_n, NEG_INF)
        _softmax_update(s_n, vn, m_s, l_s, acc)
        lam = _diff_lambda(lam_ref, lam_init)
        out = acc[...] / l_s[...]
        for h in range(N_HEADS):
            blk = out[h * 2 * t:(h + 1) * 2 * t, h * HEAD_DIM:(h + 1) * HEAD_DIM]
            o = blk[0:t] - lam * blk[t:2 * t]
            o_ref[0, :, h * HEAD_DIM:(h + 1) * HEAD_DIM] = _subln(o, sg_ref[0], lam_init)


def _diff_decode(qn, kn, proj, cache_k, cache_v, page_table, lam_all, sg_all, layer, lam_init):
    b, t, _ = qn.shape
    n_pages = page_table.shape[1]
    pps = PAGES_PER_STEP
    assert n_pages % pps == 0
    rows = N_HEADS * 2 * t
    slopes = np.repeat(_alibi_slopes(), 2 * t)
    sl_tab = jnp.asarray(np.broadcast_to(slopes[:, None], (rows, HEAD_DIM)), dtype=F32)
    grid_spec = pltpu.PrefetchScalarGridSpec(
        num_scalar_prefetch=1,
        grid=(b, n_pages // pps),
        in_specs=[
            pl.BlockSpec((rows, HEAD_DIM), lambda i, p, pt: (0, 0)),
            pl.BlockSpec((1, 4, DIFF_DQK), lambda i, p, pt: (layer, 0, 0)),
            pl.BlockSpec((1, t, BRANCH_W), lambda i, p, pt: (i, 0, 0)),
            pl.BlockSpec((1, t, BRANCH_W), lambda i, p, pt: (i, 0, 0)),
            pl.BlockSpec((1, t, BRANCH_W), lambda i, p, pt: (i, 0, C_DV)),
            pl.BlockSpec((1, 1, HEAD_DIM), lambda i, p, pt: (layer, 0, 0)),
        ] + _page_specs(layer, n_pages, pps) + _page_specs(layer, n_pages, pps),
        out_specs=pl.BlockSpec((1, t, BRANCH_W), lambda i, p, pt: (i, 0, 0)),
        scratch_shapes=[
            pltpu.VMEM((rows, BRANCH_W), BF16),
            pltpu.VMEM((rows, 1), F32),
            pltpu.VMEM((rows, 1), F32),
            pltpu.VMEM((rows, BRANCH_W), F32),
        ],
    )
    return pl.pallas_call(
        functools.partial(_diff_dec_kernel, t=t, n_pages=n_pages, pps=pps, lam_init=lam_init),
        grid_spec=grid_spec,
        out_shape=jax.ShapeDtypeStruct((b, t, BRANCH_W), F32),
        compiler_params=_cparams("parallel", "arbitrary"),
        name="diff_decode",
    )(page_table.reshape(-1), sl_tab, lam_all, qn, kn, proj, sg_all, *([cache_k] * pps), *([cache_v] * pps))


def _diag_blocks(x, t):
    return jnp.concatenate(
        [x[h * t:(h + 1) * t, h * HEAD_DIM:(h + 1) * HEAD_DIM] for h in range(N_HEADS)], axis=0)


def _moba_dec_kernel(pt_ref, sl_ref, q_ref, kn_ref, vn_ref, *rest, t, n_pages, pps):
    kc_refs, vc_refs = rest[:pps], rest[pps:2 * pps]
    o_ref, qrows, gsum, m_pg, l_pg, acc_pg = rest[2 * pps:]
    p = pl.program_id(1)
    rows = N_HEADS * t
    p_len = n_pages * PAGE
    n_blk = n_pages // 2
    scale = HEAD_DIM ** -0.5
    wide = (rows, HEAD_DIM)

    @pl.when(p == 0)
    def _():
        qrows[...] = _query_rows(q_ref[0], 1).astype(BF16)

    slope = sl_ref[:, :1]
    qpos = p_len + lax.broadcasted_iota(jnp.int32, (rows, PAGE), 0) % t
    tok = lax.broadcasted_iota(jnp.int32, (rows, PAGE), 1)

    def partial_softmax(raw, dist, mask, v_bf16):
        s = raw * scale - slope * dist
        if mask is not None:
            s = jnp.where(mask, s, NEG_INF)
        m = jnp.max(s, axis=-1, keepdims=True)
        e = jnp.exp(s - m)
        if mask is not None:
            e = jnp.where(mask, e, 0.0)
        l = jnp.sum(e, axis=-1, keepdims=True)
        pv = _diag_blocks(_dot(e.astype(BF16), v_bf16), t)
        return m, l, pv

    for k in range(pps):
        pg = p * pps + k
        raw = _dot_nt(qrows[...], kc_refs[k][0, 0].astype(BF16))
        dist = (qpos - (pg * PAGE + tok)).astype(F32)
        m, l, pv = partial_softmax(raw, dist, None, vc_refs[k][0, 0].astype(BF16))
        gsum[pg] = jnp.broadcast_to(jnp.sum(raw, axis=-1, keepdims=True), wide)
        m_pg[pg] = jnp.broadcast_to(m, wide)
        l_pg[pg] = jnp.broadcast_to(l, wide)
        acc_pg[pg] = pv

    @pl.when(p == n_pages // pps - 1)
    def _():
        pad = jnp.zeros((PAGE - t, BRANCH_W), F32)
        kn = jnp.concatenate([kn_ref[0], pad], axis=0).astype(BF16)
        vn = jnp.concatenate([vn_ref[0], pad], axis=0).astype(BF16)
        dist_n = (qpos - (p_len + tok)).astype(F32)
        mask_n = (dist_n >= 0) & (tok < t)
        m_n, l_n, pv_n = partial_softmax(_dot_nt(qrows[...], kn), dist_n, mask_n, vn)
        m_n = jnp.broadcast_to(m_n, wide)
        l_n = jnp.broadcast_to(l_n, wide)

        gates = [(gsum[2 * n] + gsum[2 * n + 1]) * (1.0 / MOBA_BLOCK) for n in range(n_blk)]
        sel = [jnp.zeros(wide, jnp.bool_) for _ in range(n_blk)]
        for _ in range(min(MOBA_TOPK, n_blk)):
            mx = functools.reduce(jnp.maximum, gates)
            found = jnp.zeros(wide, jnp.bool_)
            for n in range(n_blk):
                hit = (gates[n] == mx) & jnp.logical_not(found)
                found = found | hit
                sel[n] = sel[n] | hit
                gates[n] = jnp.where(hit, BELOW_NEG_INF, gates[n])

        m_tot = m_n
        for n in range(n_blk):
            for pg in (2 * n, 2 * n + 1):
                m_tot = jnp.maximum(m_tot, jnp.where(sel[n], m_pg[pg], NEG_INF))
        w_n = jnp.exp(m_n - m_tot)
        l_tot = l_n * w_n
        o = pv_n * w_n
        for n in range(n_blk):
            for pg in (2 * n, 2 * n + 1):
                w = jnp.where(sel[n], jnp.exp(m_pg[pg] - m_tot), 0.0)
                l_tot = l_tot + l_pg[pg] * w
                o = o + acc_pg[pg] * w
        o = o / l_tot
        for h in range(N_HEADS):
            o_ref[0, :, h * HEAD_DIM:(h + 1) * HEAD_DIM] = o[h * t:(h + 1) * t]


def _moba_decode(qn, kn, proj, cache_k, cache_v, page_table, layer):
    b, t, _ = qn.shape
    n_pages = page_table.shape[1]
    assert (n_pages * PAGE) % MOBA_BLOCK == 0 and t <= MOBA_BLOCK and MOBA_BLOCK == 2 * PAGE
    pps = PAGES_PER_STEP
    assert n_pages % pps == 0
    rows = N_HEADS * t
    slopes = np.repeat(_alibi_slopes(), t)
    sl_tab = jnp.asarray(np.broadcast_to(slopes[:, None], (rows, HEAD_DIM)), dtype=F32)
    stats = pltpu.VMEM((n_pages, rows, HEAD_DIM), F32)
    grid_spec = pltpu.PrefetchScalarGridSpec(
        num_scalar_prefetch=1,
        grid=(b, n_pages // pps),
        in_specs=[
            pl.BlockSpec((rows, HEAD_DIM), lambda i, p, pt: (0, 0)),
            pl.BlockSpec((1, t, BRANCH_W), lambda i, p, pt: (i, 0, 0)),
            pl.BlockSpec((1, t, BRANCH_W), lambda i, p, pt: (i, 0, 0)),
            pl.BlockSpec((1, t, BRANCH_W), lambda i, p, pt: (i, 0, C_MV)),
        ] + _page_specs(layer, n_pages, pps) + _page_specs(layer, n_pages, pps),
        out_specs=pl.BlockSpec((1, t, BRANCH_W), lambda i, p, pt: (i, 0, 0)),
        scratch_shapes=[pltpu.VMEM((rows, BRANCH_W), BF16), stats, stats, stats, stats],
    )
    return pl.pallas_call(
        functools.partial(_moba_dec_kernel, t=t, n_pages=n_pages, pps=pps),
        grid_spec=grid_spec,
        out_shape=jax.ShapeDtypeStruct((b, t, BRANCH_W), F32),
        compiler_params=_cparams("parallel", "arbitrary"),
        name="moba_decode",
    )(page_table.reshape(-1), sl_tab, qn, kn, proj, *([cache_k] * pps), *([cache_v] * pps))


CONV_PAD = 8


def _lru_gate_kernel(x_ref, buf_ref, cw_ref, cb_ref, wa_ref, ba_ref, wx_ref, bx_ref, lam_ref,
                     a_ref, u_ref, xe, *, tt):
    ti = pl.program_id(1)

    @pl.when(ti == 0)
    def _():
        xe[0:CONV_PAD, :] = buf_ref[0]

    @pl.when(ti > 0)
    def _():
        xe[0:CONV_PAD, :] = xe[tt:tt + CONV_PAD, :]

    xe[CONV_PAD:CONV_PAD + tt, :] = x_ref[0]
    cw = cw_ref[0]
    y = cb_ref[0]
    for i in range(CONV_W):
        off = CONV_PAD - (CONV_W - 1) + i
        y = y + xe[off:off + tt, :] * cw[i:i + 1]
    neg_lam = -lam_ref[0]
    softplus = jnp.maximum(neg_lam, 0.0) + jnp.log1p(jnp.exp(-jnp.abs(neg_lam)))
    for n in range(N_HEADS):
        sl = slice(n * HEAD_DIM, (n + 1) * HEAD_DIM)
        yn = y[:, sl]
        yb = yn.astype(BF16)
        r = _sigmoid(_dot(yb, wa_ref[0, n]) + ba_ref[0][:, sl])
        ig = _sigmoid(_dot(yb, wx_ref[0, n]) + bx_ref[0][:, sl])
        log_a = -LRU_C * r * softplus[:, sl]
        a = jnp.exp(log_a)
        a_ref[0, :, sl] = a
        u_ref[0, :, sl] = jnp.sqrt(jnp.tanh(-log_a) * (a * a + 1.0)) * (ig * yn)


def _lru_gates(proj, buf_pad, lw, layer, tt):
    b, t, _ = proj.shape
    vec = pl.BlockSpec((1, 1, BRANCH_W), lambda i, j: (layer, 0, 0))
    blk = pl.BlockSpec((1, N_HEADS, HEAD_DIM, HEAD_DIM), lambda i, j: (layer, 0, 0, 0))
    out = pl.BlockSpec((1, tt, BRANCH_W), lambda i, j: (i, j, 0))
    return pl.pallas_call(
        functools.partial(_lru_gate_kernel, tt=tt),
        grid=(b, t // tt),
        in_specs=[
            pl.BlockSpec((1, tt, BRANCH_W), lambda i, j: (i, j, C_LX)),
            pl.BlockSpec((1, CONV_PAD, BRANCH_W), lambda i, j: (i, 0, 0)),
            pl.BlockSpec((1, CONV_W, BRANCH_W), lambda i, j: (layer, 0, 0)),
            vec, blk, vec, blk, vec, vec,
        ],
        out_specs=[out, out],
        out_shape=[jax.ShapeDtypeStruct((b, t, BRANCH_W), F32)] * 2,
        scratch_shapes=[pltpu.VMEM((tt + CONV_PAD, BRANCH_W), F32)],
        compiler_params=_cparams("parallel", "arbitrary"),
        name="lru_gates",
    )(proj, buf_pad, lw["conv_w"], lw["conv_b"], lw["w_a"], lw["b_a"], lw["w_x"], lw["b_x"], lw["lam"])


def _gelu_tanh(x):
    return 0.5 * x * (1.0 + jnp.tanh(math.sqrt(2.0 / math.pi) * (x + 0.044715 * (x * x * x))))


def _lru_scan_kernel(a_ref, u_ref, g_ref, h0_ref, o_ref, ht_ref, h_scr, *, tt):
    ti = pl.program_id(1)

    @pl.when(ti == 0)
    def _():
        h_scr[...] = h0_ref[0]

    def body(i, h):
        h = a_ref[0, i] * h + u_ref[0, i]
        o_ref[0, i] = h
        return h

    h = lax.fori_loop(0, tt, body, h_scr[...], unroll=8)
    h_scr[...] = h
    ht_ref[0] = h
    o_ref[0] = o_ref[0] * _gelu_tanh(g_ref[0])


def _lru_scan(a, u, proj, h0, tt):
    b, t, _ = a.shape
    sub = BRANCH_W // HEAD_DIM
    a4 = a.reshape(b, t, sub, HEAD_DIM)
    u4 = u.reshape(b, t, sub, HEAD_DIM)
    g4 = proj[:, :, C_LG * BRANCH_W:(C_LG + 1) * BRANCH_W].reshape(b, t, sub, HEAD_DIM)
    blk = pl.BlockSpec((1, tt, sub, HEAD_DIM), lambda i, j: (i, j, 0, 0))
    out, h_t = pl.pallas_call(
        functools.partial(_lru_scan_kernel, tt=tt),
        grid=(b, t // tt),
        in_specs=[
            blk, blk, blk,
            pl.BlockSpec((1, sub, HEAD_DIM), lambda i, j: (i, 0, 0)),
        ],
        out_specs=[blk, pl.BlockSpec((1, sub, HEAD_DIM), lambda i, j: (i, 0, 0))],
        out_shape=[
            jax.ShapeDtypeStruct((b, t, sub, HEAD_DIM), F32),
            jax.ShapeDtypeStruct((b, sub, HEAD_DIM), F32),
        ],
        scratch_shapes=[pltpu.VMEM((sub, HEAD_DIM), F32)],
        compiler_params=_cparams("parallel", "arbitrary"),
        name="lru_scan",
    )(a4, u4, g4, h0.reshape(b, sub, HEAD_DIM))
    return out.reshape(b, t, BRANCH_W), h_t.reshape(b, BRANCH_W)


def _trunk_layer(x, mod, w, layer, lam_init, state, past, cfg):
    b, t, d = x.shape
    m = b * t
    shift1, scale1, gate1, shift2, scale2, gate2 = mod
    xm = x.reshape(cfg["mod_shape"])

    u = _normmod(xm, w["norm1_g"], layer, scale1, shift1, cfg["tt_norm"]).reshape(m, d)
    proj2 = _matmul(u, w["w_in"], layer, cfg["tm"], cfg["tn_in"])
    proj = proj2.reshape(b, t, proj2.shape[1])

    ret_s0, lru_h0, conv0 = state
    o_ret, ret_s = _retention(proj, ret_s0, w["ret_gn_g"], layer, cfg["ret_chunk"])

    tt = cfg["tt_head"]
    dq = _headnorm(proj, C_DQ, w["diff_qn_g"], layer, DIFF_DQK, tt)
    dk, dk_cache, dv_cache = _kv_prep(proj, C_DK, C_DV, w["diff_kn_g"], layer, DIFF_DQK, tt)
    mq = _headnorm(proj, C_MQ, w["moba_qn_g"], layer, HEAD_DIM, tt)
    mk, mk_cache, mv_cache = _kv_prep(proj, C_MK, C_MV, w["moba_kn_g"], layer, HEAD_DIM, tt)
    if past is None:
        o_dif = _diff_attn(dq, dk, proj, w["diff_lam"], w["diff_subln_g"], layer, lam_init, cfg["tq_diff"])
        o_moba = _moba_attn(mq, mk, proj)
    else:
        cdk, cdv, cmk, cmv, page_table = past
        o_dif = _diff_decode(dq, dk, proj, cdk, cdv, page_table, w["diff_lam"], w["diff_subln_g"], layer, lam_init)
        o_moba = _moba_decode(mq, mk, proj, cmk, cmv, page_table, layer)

    buf_pad = jnp.pad(conv0, ((0, 0), (CONV_PAD - (CONV_W - 1), 0), (0, 0)))
    a, uu = _lru_gates(proj, buf_pad, w["lru"], layer, cfg["tt_lru"])
    o_lru, lru_h = _lru_scan(a, uu, proj, lru_h0, cfg["tt_lru"])

    branch_outs = [o.reshape(m, BRANCH_W) for o in (o_ret, o_dif, o_moba, o_lru)]
    merged = _merge(branch_outs, proj2, w["w_br"], layer, cfg["tm_merge"], cfg["tn_merge"])
    x2 = _matmul_resid(merged, w["w_o"], layer, x.reshape(m, d), gate1, cfg["rows_per_gate"],
                       cfg["tm"], cfg["tn_out"])
    u2 = _normmod(x2.reshape(cfg["mod_shape"]), w["norm2_g"], layer, scale2, shift2, cfg["tt_norm"]).reshape(m, d)
    act = _matmul_swiglu(u2, w["w_ffn_in"], layer, cfg["tm"], cfg["tn_ffn"])
    x3 = _matmul_resid_ksplit(act, w["w_ffn_out"], layer, x2, gate2, cfg["rows_per_gate"],
                              cfg["tm"], cfg["tn_ffn_out"], cfg["tk_ffn"])

    lx = proj[:, :, C_LX * BRANCH_W:(C_LX + 1) * BRANCH_W]
    lru_conv = jnp.concatenate([conv0, lx], axis=1)[:, -(CONV_W - 1):]
    new_state = (dk_cache, dv_cache, mk_cache, mv_cache, ret_s, lru_h, lru_conv)
    return x3.reshape(b, t, d), new_state


def _prompt_cfg(b, t, d, d_ff):
    return dict(
        mod_shape=(b, t, d), tt_norm=256, tm=1024, tn_in=512, tn_out=512, tn_ffn=256,
        tn_ffn_out=512, tk_ffn=d_ff // 2,
        ret_chunk=128 if t % 128 == 0 else t, tt_head=512, tq_diff=256, tt_lru=512,
        tm_merge=512, tn_merge=512, rows_per_gate=t,
    )


def _sample_cfg(b, t, d, d_ff):
    m = b * t
    return dict(
        mod_shape=(1, m, d), tt_norm=m, tm=m, tn_in=1024, tn_out=1024, tn_ffn=256,
        tn_ffn_out=1024, tk_ffn=d_ff // 2,
        ret_chunk=128 if t % 128 == 0 else t, tt_head=t, tt_lru=t,
        tm_merge=m, tn_merge=1024, rows_per_gate=m,
    )


def kernel(x_prompt, x_sample, c_prompt, c_sample, cache_dif_k, cache_dif_v, cache_moba_k, cache_moba_v, state_ret, state_lru_h, state_lru_conv, page_table, norm1_g, norm2_g, w_ada, b_ada, w_in, ret_gn_g, diff_qn_g, diff_kn_g, diff_lam, diff_subln_g, moba_qn_g, moba_kn_g, lru_conv_w, lru_conv_b, lru_w_a, lru_b_a, lru_w_x, lru_b_x, lru_lam, w_br, w_o, w_ffn_in, w_ffn_out):
    depth, d = norm1_g.shape
    bp, tp, _ = x_prompt.shape
    bs, ts, _ = x_sample.shape
    d_ff = w_ffn_out.shape[1]
    n_pool = cache_dif_k.shape[1]

    def vec(a):
        return a.reshape(depth, 1, a.shape[-1])

    w = dict(
        norm1_g=vec(norm1_g), norm2_g=vec(norm2_g),
        w_in=w_in, w_br=w_br, w_o=w_o, w_ffn_in=w_ffn_in, w_ffn_out=w_ffn_out.astype(BF16),
        ret_gn_g=vec(ret_gn_g),
        diff_qn_g=vec(jnp.tile(diff_qn_g, (1, 2))), diff_kn_g=vec(jnp.tile(diff_kn_g, (1, 2))),
        diff_lam=diff_lam, diff_subln_g=vec(diff_subln_g),
        moba_qn_g=vec(moba_qn_g), moba_kn_g=vec(moba_kn_g),
        lru=dict(conv_w=lru_conv_w, conv_b=vec(lru_conv_b), w_a=lru_w_a.astype(BF16), b_a=vec(lru_b_a),
                 w_x=lru_w_x.astype(BF16), b_x=vec(lru_b_x), lam=vec(lru_lam)),
    )

    n_c = bp + bs
    rows = -(-n_c // 8) * 8
    c_all = jnp.pad(jnp.concatenate([c_prompt, c_sample], axis=0), ((0, rows - n_c), (0, 0)))
    mod_all = _ada_mod(c_all, w_ada, b_ada).reshape(depth, rows, N_ADA, d)

    flat = lambda a: a.reshape(a.shape[0], a.shape[1], PAGE, BRANCH_W)
    cdk, cdv, cmk, cmv = flat(cache_dif_k), flat(cache_dif_v), flat(cache_moba_k), flat(cache_moba_v)

    cfg_p = _prompt_cfg(bp, tp, d, d_ff)
    cfg_s = _sample_cfg(bs, ts, d, d_ff)
    zero_state = (jnp.zeros((bp, N_HEADS, HEAD_DIM, HEAD_DIM), F32), jnp.zeros((bp, BRANCH_W), F32),
                  jnp.zeros((bp, CONV_W - 1, BRANCH_W), F32))

    y_p, y_s = x_prompt, x_sample
    new_p = [[] for _ in range(7)]
    new_s = [[] for _ in range(7)]
    for l in range(depth):
        lam_init = 0.8 - 0.6 * math.exp(-0.3 * l)
        mod_p = [mod_all[l, :bp, i][:, None, :] for i in range(N_ADA)]
        mod_s = [jnp.repeat(mod_all[l, bp:n_c, i], ts, axis=0)[None] for i in range(N_ADA)]
        y_p, st_p = _trunk_layer(y_p, mod_p, w, l, lam_init, zero_state, None, cfg_p)
        y_s, st_s = _trunk_layer(y_s, mod_s, w, l, lam_init,
                                 (state_ret[l], state_lru_h[l], state_lru_conv[l]),
                                 (cdk, cdv, cmk, cmv, page_table), cfg_s)
        for acc, a in zip(new_p, st_p):
            acc.append(a)
        for acc, a in zip(new_s, st_s):
            acc.append(a)
    outs_p = [jnp.stack(a) for a in new_p]
    outs_s = [jnp.stack(a) for a in new_s]
    return (y_p, y_s, *outs_p, *outs_s)
```
